```python
import math
import jax, jax.numpy as jnp
from jax import lax
import numpy as np

D_MODEL = 4096
BATCH = 2
SEQ = 4096
DEPTH = 4

CTX_LEN = 256
GRID_W = 64
Q_BLOCK = 128
HEAD_DIM = 128
A_HEADS = D_MODEL // (4 * HEAD_DIM)
A_QK_DIM = HEAD_DIM // 2
B_HEADS = 3 * D_MODEL // (8 * HEAD_DIM)
B_KV_HEADS = B_HEADS // 3
C_HEADS = 3 * D_MODEL // (8 * HEAD_DIM)
NA_ROWS = 8
NA_COLS = 16
A_WIDTH = A_HEADS * HEAD_DIM
B_WIDTH = B_HEADS * HEAD_DIM
B_KV_WIDTH = B_KV_HEADS * HEAD_DIM
C_WIDTH = C_HEADS * HEAD_DIM
MIX_WIDTH = A_WIDTH + B_WIDTH + C_WIDTH
IN_SPLITS = (A_WIDTH, A_WIDTH, A_WIDTH, B_WIDTH, B_KV_WIDTH, B_KV_WIDTH, C_WIDTH, C_WIDTH, C_WIDTH)
IN_HEADS = (A_HEADS, A_HEADS, A_HEADS, B_HEADS, B_KV_HEADS, B_KV_HEADS, C_HEADS, C_HEADS, C_HEADS)
IN_WIDTH = sum(IN_SPLITS)
N_EXPERTS = 16
EC_CAPACITY = 2
EXPERT_FF = 384
N_MOD = 6
ROPE_BASE = 10000.0
EPS = 1e-6

kernel_name = 'hybrid_diffusion_block'


def rms_norm(x, g):
    xf = x.astype(jnp.float32)
    y = xf * lax.rsqrt(jnp.mean(xf * xf, axis=-1, keepdims=True) + EPS)
    return (y * g.astype(jnp.float32)).astype(x.dtype)


def rope_1d(x, pos):
    half = x.shape[-1] // 2
    freqs = ROPE_BASE ** (-jnp.arange(half, dtype=jnp.float32) / half)
    ang = pos.astype(jnp.float32)[:, None] * freqs
    cos, sin = jnp.cos(ang)[:, None, :], jnp.sin(ang)[:, None, :]
    xf = x.astype(jnp.float32)
    x1, x2 = xf[..., :half], xf[..., half:]
    return jnp.concatenate([x1 * cos - x2 * sin, x2 * cos + x1 * sin], axis=-1).astype(x.dtype)


def axial_rope(x, row, col):
    half = x.shape[-1] // 2
    return jnp.concatenate([rope_1d(x[..., :half], row), rope_1d(x[..., half:], col)], axis=-1)


def rope_halves(x, row, col):
    return jnp.concatenate([axial_rope(x[..., :A_QK_DIM], row, col),
                            axial_rope(x[..., A_QK_DIM:], row, col)], axis=-1)


def modulate(h, shift, scale):
    return h * (1.0 + scale) + shift


def project_heads(h, w_in):
    p = h @ w_in
    offs = np.cumsum(IN_SPLITS)[:-1].tolist()
    parts = jnp.split(p, offs, axis=-1)
    return [pp.reshape(*pp.shape[:-1], nh, HEAD_DIM) for pp, nh in zip(parts, IN_HEADS)]


def sweep_query_blocks(fn, q):
    b, n = q.shape[:2]
    nb = n // Q_BLOCK
    qb = jnp.moveaxis(q.reshape(b, nb, Q_BLOCK, *q.shape[2:]), 1, 0)
    out = lax.map(fn, qb)
    return jnp.moveaxis(out, 0, 1).reshape(b, n, *out.shape[3:])


def diff_attention(q, k, v, lam, subln_g, lam_init):
    scale = A_QK_DIM ** -0.5
    s1 = jnp.einsum('bqhd,bkhd->bhqk', q[..., :A_QK_DIM], k[..., :A_QK_DIM]).astype(jnp.float32) * scale
    s2 = jnp.einsum('bqhd,bkhd->bhqk', q[..., A_QK_DIM:], k[..., A_QK_DIM:]).astype(jnp.float32) * scale
    p = jax.nn.softmax(s1, axis=-1) - lam * jax.nn.softmax(s2, axis=-1)
    o = jnp.einsum('bhqk,bkhd->bqhd', p.astype(v.dtype), v)
    return rms_norm(o, subln_g) * (1.0 - lam_init)


def gqa_attention(q, k, v):
    b, nq, h, d = q.shape
    kvh = k.shape[2]
    qg = q.reshape(b, nq, kvh, h // kvh, d)
    s = jnp.einsum('bqhgd,bkhd->bhgqk', qg, k).astype(jnp.float32) * (d ** -0.5)
    p = jax.nn.softmax(s, axis=-1).astype(v.dtype)
    o = jnp.einsum('bhgqk,bkhd->bqhgd', p, v)
    return o.reshape(b, nq, h, d)


def neighbourhood_attention(q, k, v, k_ctx, v_ctx, rpb, rows):
    b, n, h, d = q.shape
    kh = min(NA_ROWS, rows)
    scale = d ** -0.5
    qg = q.reshape(b, rows, GRID_W, h, d)
    kg = k.reshape(b, rows, GRID_W, h, d)
    vg = v.reshape(b, rows, GRID_W, h, d)
    qcol = jnp.arange(GRID_W)
    col_idx = jnp.clip(qcol - NA_COLS // 2, 0, GRID_W - NA_COLS)[:, None] + jnp.arange(NA_COLS)
    bias_cols = rpb[:, :, col_idx - qcol[:, None] + NA_COLS - 1]

    def one_row(r):
        rs = jnp.clip(r - kh // 2, 0, rows - kh)
        q_r = lax.dynamic_index_in_dim(qg, r, axis=1, keepdims=False)
        k_r = lax.dynamic_slice_in_dim(kg, rs, kh, axis=1)[:, :, col_idx]
        v_r = lax.dynamic_slice_in_dim(vg, rs, kh, axis=1)[:, :, col_idx]
        bias = bias_cols[:, rs + jnp.arange(kh) - r + NA_ROWS - 1]
        s_loc = (jnp.einsum('bqhd,bkqwhd->bhqkw', q_r, k_r).astype(jnp.float32) * scale
                 + jnp.transpose(bias, (0, 2, 1, 3)).astype(jnp.float32)[None])
        s_ctx = jnp.einsum('bqhd,bkhd->bhqk', q_r, k_ctx).astype(jnp.float32) * scale
        n_loc = kh * NA_COLS
        p = jax.nn.softmax(jnp.concatenate([s_loc.reshape(b, h, GRID_W, n_loc), s_ctx], axis=-1), axis=-1)
        p = p.astype(v.dtype)
        p_loc = p[..., :n_loc].reshape(b, h, GRID_W, kh, NA_COLS)
        return (jnp.einsum('bhqkw,bkqwhd->bqhd', p_loc, v_r)
                + jnp.einsum('bhqk,bkhd->bqhd', p[..., n_loc:], v_ctx))

    out = lax.map(one_row, jnp.arange(rows))
    return jnp.moveaxis(out, 0, 1).reshape(b, n, h, d)


def merge_heads(oa, ob, oc, w_out):
    b, n = oa.shape[:2]
    return jnp.concatenate([oa.reshape(b, n, -1), ob.reshape(b, n, -1), oc.reshape(b, n, -1)], axis=-1) @ w_out


def token_mixer(hx, hc, w_in, w_out, a_lambda, a_subln_g, b_q_norm_g, b_k_norm_g, c_rpb,
                lam_init, row, col, rows, need_ctx):
    qa_x, ka_x, va_x, qb_x, kb_x, vb_x, qc_x, kc_x, vc_x = project_heads(hx, w_in)
    qa_c, ka_c, va_c, qb_c, kb_c, vb_c, qc_c, kc_c, vc_c = project_heads(hc, w_in)

    lv = a_lambda.astype(jnp.float32)
    lam = jnp.exp(jnp.sum(lv[0] * lv[1])) - jnp.exp(jnp.sum(lv[2] * lv[3])) + lam_init
    ka_all = jnp.concatenate([ka_c, rope_halves(ka_x, row, col)], axis=1)
    va_all = jnp.concatenate([va_c, va_x], axis=1)
    oa_x = sweep_query_blocks(lambda qblk: diff_attention(qblk, ka_all, va_all, lam, a_subln_g, lam_init),
                              rope_halves(qa_x, row, col))

    kb_c = rms_norm(kb_c, b_k_norm_g)
    kb_all = jnp.concatenate([kb_c, axial_rope(rms_norm(kb_x, b_k_norm_g), row, col)], axis=1)
    vb_all = jnp.concatenate([vb_c, vb_x], axis=1)
    ob_x = sweep_query_blocks(lambda qblk: gqa_attention(qblk, kb_all, vb_all),
                              axial_rope(rms_norm(qb_x, b_q_norm_g), row, col))

    oc_x = neighbourhood_attention(qc_x, kc_x, vc_x, kc_c, vc_c, c_rpb, rows)

    out_x = merge_heads(oa_x, ob_x, oc_x, w_out)
    if not need_ctx:
        return out_x, None
    oa_c = diff_attention(qa_c, ka_c, va_c, lam, a_subln_g, lam_init)
    ob_c = gqa_attention(rms_norm(qb_c, b_q_norm_g), kb_c, vb_c)
    oc_c = gqa_attention(qc_c, kc_c, vc_c)
    return out_x, merge_heads(oa_c, ob_c, oc_c, w_out)


def expert_choice_ffn(h, w_router, w_gate, w_up, w_down):
    b, n, _ = h.shape
    cap = EC_CAPACITY * n // N_EXPERTS
    aff = jax.nn.softmax((h @ w_router).astype(jnp.float32), axis=-1)
    gates, idx = lax.top_k(jnp.swapaxes(aff, 1, 2), cap)
    bidx = jnp.arange(b)[:, None, None]
    xg = h[bidx, idx]
    hid = jax.nn.silu(jnp.einsum('becd,edf->becf', xg, w_gate)) * jnp.einsum('becd,edf->becf', xg, w_up)
    ye = jnp.einsum('becf,efd->becd', hid, w_down) * gates[..., None].astype(h.dtype)
    return jnp.zeros_like(h).at[bidx, idx].add(ye)


def setup_inputs(seed: int = 0) -> dict:
    key = jax.random.key(seed)
    ks = jax.random.split(key, 20)
    D = D_MODEL

    def nrm(k, shape, scale):
        return jax.random.normal(k, shape, jnp.float32) * scale

    return {
        'x': nrm(ks[0], (BATCH, SEQ, D), 1.0),
        'c': nrm(ks[1], (BATCH, D), 1.0),
        'ctx': nrm(ks[2], (BATCH, CTX_LEN, D), 1.0),
        'c_ctx': nrm(ks[3], (D,), 1.0),
        'w_ada': nrm(ks[4], (DEPTH, D, N_MOD * D), 0.5 * D ** -0.5),
        'b_ada': nrm(ks[5], (DEPTH, N_MOD * D), 0.01),
        'norm1_g': 1.0 + nrm(ks[6], (DEPTH, D), 0.01),
        'norm2_g': 1.0 + nrm(ks[7], (DEPTH, D), 0.01),
        'w_in': nrm(ks[8], (DEPTH, D, IN_WIDTH), D ** -0.5),
        'w_out': nrm(ks[9], (DEPTH, MIX_WIDTH, D), MIX_WIDTH ** -0.5),
        'a_lambda': nrm(ks[10], (DEPTH, 4, A_QK_DIM), 0.1),
        'a_subln_g': 1.0 + nrm(ks[11], (DEPTH, HEAD_DIM), 0.01),
        'b_q_norm_g': 1.0 + nrm(ks[12], (DEPTH, HEAD_DIM), 0.01),
        'b_k_norm_g': 1.0 + nrm(ks[13], (DEPTH, HEAD_DIM), 0.01),
        'c_rpb': nrm(ks[14], (DEPTH, C_HEADS, 2 * NA_ROWS - 1, 2 * NA_COLS - 1), 0.1),
        'w_router': nrm(ks[15], (DEPTH, D, N_EXPERTS), D ** -0.5),
        'w_e_gate': nrm(ks[16], (DEPTH, N_EXPERTS, D, EXPERT_FF), D ** -0.5),
        'w_e_up': nrm(ks[17], (DEPTH, N_EXPERTS, D, EXPERT_FF), D ** -0.5),
        'w_e_down': nrm(ks[18], (DEPTH, N_EXPERTS, EXPERT_FF, D), EXPERT_FF ** -0.5),
        'final_g': 1.0 + nrm(ks[19], (D,), 0.01),
    }


def reference(x, c, ctx, c_ctx, w_ada, b_ada, norm1_g, norm2_g, w_in, w_out, a_lambda, a_subln_g,
              b_q_norm_g, b_k_norm_g, c_rpb, w_router, w_e_gate, w_e_up, w_e_down, final_g):
    n = x.shape[1]
    rows = n // GRID_W
    t = jnp.arange(n)
    row, col = t // GRID_W, t % GRID_W
    silu_c = jax.nn.silu(c)
    silu_cc = jax.nn.silu(c_ctx)[None]
    for l in range(DEPTH):
        need_ctx = l < DEPTH - 1
        lam_init = 0.8 - 0.6 * math.exp(-0.3 * l)
        sh1, sc1, g1, sh2, sc2, g2 = jnp.split((silu_c @ w_ada[l] + b_ada[l])[:, None, :], N_MOD, axis=-1)
        csh1, csc1, cg1, csh2, csc2, cg2 = jnp.split((silu_cc @ w_ada[l] + b_ada[l])[:, None, :], N_MOD, axis=-1)
        hx = modulate(rms_norm(x, norm1_g[l]), sh1, sc1)
        hc = modulate(rms_norm(ctx, norm1_g[l]), csh1, csc1)
        ox, oc = token_mixer(hx, hc, w_in[l], w_out[l], a_lambda[l], a_subln_g[l], b_q_norm_g[l],
                             b_k_norm_g[l], c_rpb[l], lam_init, row, col, rows, need_ctx)
        x = x + g1 * ox
        hx = modulate(rms_norm(x, norm2_g[l]), sh2, sc2)
        x = x + g2 * expert_choice_ffn(hx, w_router[l], w_e_gate[l], w_e_up[l], w_e_down[l])
        if need_ctx:
            ctx = ctx + cg1 * oc
            hc = modulate(rms_norm(ctx, norm2_g[l]), csh2, csc2)
            ctx = ctx + cg2 * expert_choice_ffn(hc, w_router[l], w_e_gate[l], w_e_up[l], w_e_down[l])
    return rms_norm(x, final_g)
```

```python
import functools

import numpy as np
import jax
import jax.numpy as jnp
from jax import lax
from jax.experimental import pallas as pl
from jax.experimental.pallas import tpu as pltpu

HEAD_DIM = 128
A_QK_DIM = HEAD_DIM // 2
GRID_W = 64
NA_ROWS = 8
NA_COLS = 16
EC_CAPACITY = 2
N_MOD = 6
ROPE_BASE = 10000.0
EPS = 1e-6
NEG = -1e30
MIB = 1024 * 1024
MOD_ROWS = 8
F32 = jnp.float32
BF16 = jnp.bfloat16

UNIT_AQ, UNIT_AK, UNIT_AV = 0, 2, 4
UNIT_BQ, UNIT_BK, UNIT_BV = 6, 9, 10
UNIT_CQ, UNIT_CK, UNIT_CV = 11, 14, 17
N_UNITS = 20


def _cparams(sem, vmem_mib):
    return pltpu.CompilerParams(dimension_semantics=sem, vmem_limit_bytes=vmem_mib * MIB)


def _nt_dot(a, b):
    return lax.dot_general(a, b, (((1,), (1,)), ((), ())), preferred_element_type=F32)


def _ada_kernel(c_ref, w_ref, b_ref, o_ref):
    c = c_ref[...]
    s = (c * jax.nn.sigmoid(c)).astype(BF16)
    o_ref[...] = jnp.dot(s, w_ref[...].astype(BF16), preferred_element_type=F32) + b_ref[...]


def _ada_mod(cvec, w_ada, b_ada):
    depth, d, nm = w_ada.shape
    tn = min(512, nm)
    return pl.pallas_call(
        _ada_kernel,
        grid=(depth, nm // tn),
        in_specs=[pl.BlockSpec((MOD_ROWS, d), lambda l, j: (0, 0)),
                  pl.BlockSpec((None, d, tn), lambda l, j: (l, 0, j)),
                  pl.BlockSpec((None, 1, tn), lambda l, j: (l, 0, j))],
        out_specs=pl.BlockSpec((None, MOD_ROWS, tn), lambda l, j: (l, 0, j)),
        out_shape=jax.ShapeDtypeStruct((depth, MOD_ROWS, nm), F32),
        compiler_params=_cparams(("parallel", "parallel"), 40),
    )(cvec, w_ada, b_ada.reshape(depth, 1, nm))


def _rms(x, g):
    return x * lax.rsqrt(jnp.mean(x * x, axis=-1, keepdims=True) + EPS) * g


def _norm_mod_kernel(x_ref, g_ref, sh_ref, sc_ref, o_ref):
    y = _rms(x_ref[...], g_ref[...])
    o_ref[...] = (y * (1.0 + sc_ref[...]) + sh_ref[...]).astype(o_ref.dtype)


def _norm_kernel(x_ref, g_ref, o_ref):
    o_ref[...] = _rms(x_ref[...], g_ref[...]).astype(o_ref.dtype)


def _row_spec(d, row_fn):
    return pl.BlockSpec((None, 1, d), lambda b, i: (row_fn(b), 0, 0))


def _norm_mod(x, g, shift, scale, row_fn, out_dtype):
    bsz, n, d = x.shape
    tm = min(256, n)
    return pl.pallas_call(
        _norm_mod_kernel,
        grid=(bsz, n // tm),
        in_specs=[pl.BlockSpec((None, tm, d), lambda b, i: (b, i, 0)),
                  pl.BlockSpec((1, d), lambda b, i: (0, 0)),
                  _row_spec(d, row_fn), _row_spec(d, row_fn)],
        out_specs=pl.BlockSpec((None, tm, d), lambda b, i: (b, i, 0)),
        out_shape=jax.ShapeDtypeStruct((bsz, n, d), out_dtype),
        compiler_params=_cparams(("parallel", "parallel"), 32),
    )(x, g.reshape(1, d), shift, scale)


def _final_norm(x, g):
    bsz, n, d = x.shape
    tm = min(256, n)
    return pl.pallas_call(
        _norm_kernel,
        grid=(bsz, n // tm),
        in_specs=[pl.BlockSpec((None, tm, d), lambda b, i: (b, i, 0)),
                  pl.BlockSpec((1, d), lambda b, i: (0, 0))],
        out_specs=pl.BlockSpec((None, tm, d), lambda b, i: (b, i, 0)),
        out_shape=jax.ShapeDtypeStruct((bsz, n, d), x.dtype),
        compiler_params=_cparams(("parallel", "parallel"), 32),
    )(x, g.reshape(1, d))


def _rope_tables(n, group, identity):
    if identity:
        return jnp.ones((n, HEAD_DIM), F32), jnp.zeros((n, HEAD_DIM), F32)
    p = jnp.arange(HEAD_DIM)
    half = 16 if group == "A" else 32
    use_col = ((p // (2 * half)) % 2) == 1
    first = (p % (2 * half)) < half
    freqs = ROPE_BASE ** (-(p % half).astype(F32) / half)
    t = jnp.arange(n)
    pos = jnp.where(use_col[None, :], (t % GRID_W)[:, None], (t // GRID_W)[:, None]).astype(F32)
    ang = pos * freqs[None, :]
    return jnp.cos(ang), jnp.where(first[None, :], -jnp.sin(ang), jnp.sin(ang))


def _rotate(x, cos, sin_signed, half):
    lane = lax.broadcasted_iota(jnp.int32, x.shape, 1)
    first = (lane % (2 * half)) < half
    partner = jnp.where(first, pltpu.roll(x, HEAD_DIM - half, 1), pltpu.roll(x, half, 1))
    return x * cos + partner * sin_signed


def _inproj_kernel(h_ref, w_ref, ca_ref, sa_ref, cb_ref, sb_ref, gq_ref, gk_ref, o_ref):
    j = pl.program_id(1)
    acc = jnp.dot(h_ref[...], w_ref[...], preferred_element_type=F32)
    nh = acc.shape[1] // HEAD_DIM
    heads = [slice(i * HEAD_DIM, (i + 1) * HEAD_DIM) for i in range(nh)]

    @pl.when(j < UNIT_AV)
    def _():
        scale = jnp.where(j < UNIT_AK, A_QK_DIM ** -0.5, 1.0).astype(F32)
        for s in heads:
            o_ref[:, s] = (_rotate(acc[:, s], ca_ref[...], sa_ref[...], 16) * scale).astype(o_ref.dtype)

    @pl.when((j >= UNIT_BQ) & (j < UNIT_BV))
    def _():
        is_q = j < UNIT_BK
        g = jnp.where(is_q, gq_ref[...], gk_ref[...])
        scale = jnp.where(is_q, HEAD_DIM ** -0.5, 1.0).astype(F32)
        for s in heads:
            y = _rms(acc[:, s], g)
            o_ref[:, s] = (_rotate(y, cb_ref[...], sb_ref[...], 32) * scale).astype(o_ref.dtype)

    @pl.when((j >= UNIT_CQ) & (j < UNIT_CK))
    def _():
        o_ref[...] = (acc * (HEAD_DIM ** -0.5)).astype(o_ref.dtype)

    @pl.when(((j >= UNIT_AV) & (j < UNIT_BQ)) | (j == UNIT_BV) | (j >= UNIT_CK))
    def _():
        o_ref[...] = acc.astype(o_ref.dtype)


def _in_proj(h, w, tabs_a, tabs_b, gq, gk, n_per_sample):
    m, d = h.shape
    wid = w.shape[1]
    tn = wid // N_UNITS
    tm = min(1024, n_per_sample)
    tiles = n_per_sample // tm
    tab = pl.BlockSpec((tm, HEAD_DIM), lambda i, j: (i % tiles, 0))
    vec = pl.BlockSpec((1, HEAD_DIM), lambda i, j: (0, 0))
    return pl.pallas_call(
        _inproj_kernel,
        grid=(m // tm, N_UNITS),
        in_specs=[pl.BlockSpec((tm, d), lambda i, j: (i, 0)),
                  pl.BlockSpec((d, tn), lambda i, j: (0, j)),
                  tab, tab, tab, tab, vec, vec],
        out_specs=pl.BlockSpec((tm, tn), lambda i, j: (i, j)),
        out_shape=jax.ShapeDtypeStruct((m, wid), BF16),
        compiler_params=_cparams(("parallel", "arbitrary"), 48),
    )(h, w, tabs_a[0], tabs_a[1], tabs_b[0], tabs_b[1], gq.reshape(1, HEAD_DIM), gk.reshape(1, HEAD_DIM))


def _flash_kernel(*refs, diff, has_lat, tk, lam_init):
    it = iter(refs)
    q_ref, kc_ref, vc_ref = next(it), next(it), next(it)
    kl_ref = vl_ref = lam_ref = g_ref = None
    if has_lat:
        kl_ref, vl_ref = next(it), next(it)
    if diff:
        lam_ref, g_ref = next(it), next(it)
    o_ref = next(it)

    q = q_ref[...]
    tq = q.shape[0]
    if diff:
        qf = q.astype(F32)
        lane = lax.broadcasted_iota(jnp.int32, qf.shape, 1)
        qs = [jnp.where(lane < A_QK_DIM, qf, 0.0).astype(BF16), jnp.where(lane >= A_QK_DIM, qf, 0.0).astype(BF16)]
    else:
        qs = [q]

    def step(k, v, state):
        new = []
        for qq, (m, l, acc) in zip(qs, state):
            s = _nt_dot(qq, k)
            m_new = jnp.maximum(m, jnp.max(s, axis=-1, keepdims=True))
            a = jnp.exp(m - m_new)
            p = jnp.exp(s - m_new)
            l = a * l + jnp.sum(p, axis=-1, keepdims=True)
            acc = a * acc + jnp.dot(p.astype(BF16), v, preferred_element_type=F32)
            new.append((m_new, l, acc))
        return tuple(new)

    init = tuple((jnp.full((tq, 1), NEG, F32), jnp.zeros((tq, 1), F32), jnp.zeros((tq, HEAD_DIM), F32)) for _ in qs)
    state = step(kc_ref[...], vc_ref[...], init)
    if has_lat:
        def body(c, st):
            off = pl.multiple_of(c * tk, tk)
            return step(kl_ref[pl.ds(off, tk), :], vl_ref[pl.ds(off, tk), :], st)
        state = lax.fori_loop(0, kl_ref.shape[0] // tk, body, state)

    if diff:
        lv = lam_ref[...]
        lam = (jnp.exp(jnp.sum(lv[0:1] * lv[1:2], axis=-1, keepdims=True))
               - jnp.exp(jnp.sum(lv[2:3] * lv[3:4], axis=-1, keepdims=True)) + lam_init)
        (_, l1, a1), (_, l2, a2) = state
        o = a1 / l1 - lam * (a2 / l2)
        o = _rms(o, g_ref[...]) * (1.0 - lam_init)
    else:
        (_, l1, a1), = state
        o = a1 / l1
    o_ref[...] = o.astype(o_ref.dtype)


def _flash(pq, pc, px, *, n_q, n_ctx, n_lat, heads, kv_group, q_unit, k_unit, v_unit, unit_heads,
           diff=False, lam=None, subln_g=None, lam_init=0.0):
    bsz = pq.shape[0] // n_q
    has_lat = px is not None
    tq = min(256, n_q)
    tk = min(512, n_lat) if has_lat else 0
    nqt = n_q // tq
    qc, kc, vc = q_unit * unit_heads, k_unit * unit_heads, v_unit * unit_heads
    in_specs = [pl.BlockSpec((tq, HEAD_DIM), lambda b, h, i: (b * nqt + i, qc + h)),
                pl.BlockSpec((n_ctx, HEAD_DIM), lambda b, h, i: (b, kc + h // kv_group)),
                pl.BlockSpec((n_ctx, HEAD_DIM), lambda b, h, i: (b, vc + h // kv_group))]
    args = [pq, pc, pc]
    if has_lat:
        in_specs += [pl.BlockSpec((n_lat, HEAD_DIM), lambda b, h, i: (b, kc + h // kv_group)),
                     pl.BlockSpec((n_lat, HEAD_DIM), lambda b, h, i: (b, vc + h // kv_group))]
        args += [px, px]
    if diff:
        in_specs += [pl.BlockSpec((4, A_QK_DIM), lambda b, h, i: (0, 0)),
                     pl.BlockSpec((1, HEAD_DIM), lambda b, h, i: (0, 0))]
        args += [lam, subln_g.reshape(1, HEAD_DIM)]
    return pl.pallas_call(
        functools.partial(_flash_kernel, diff=diff, has_lat=has_lat, tk=tk, lam_init=lam_init),
        grid=(bsz, heads, nqt),
        in_specs=in_specs,
        out_specs=pl.BlockSpec((tq, HEAD_DIM), lambda b, h, i: (b * nqt + i, h)),
        out_shape=jax.ShapeDtypeStruct((bsz * n_q, heads * HEAD_DIM), BF16),
        compiler_params=_cparams(("parallel", "parallel", "arbitrary"), 32),
    )(*args)


def _na_bias_table(rpb, rows):
    kh = min(NA_ROWS, rows)
    v = np.arange(kh)[:, None]
    kr = np.arange(kh)[None, :]
    ri = kr - v + NA_ROWS - 1
    q = np.arange(GRID_W)
    start = np.clip(q - NA_COLS // 2, 0, GRID_W - NA_COLS)
    kc = np.arange(GRID_W)
    inwin = (kc[None, :] >= start[:, None]) & (kc[None, :] < start[:, None] + NA_COLS)
    ci = np.clip(kc[None, :] - q[:, None] + NA_COLS - 1, 0, 2 * NA_COLS - 2)
    t = rpb[:, ri[:, None, :, None], ci[None, :, None, :]]
    t = jnp.where(jnp.asarray(inwin)[None, None, :, None, :], t.astype(F32), NEG)
    return t.reshape(rpb.shape[0], kh, GRID_W, kh * GRID_W)


def _na_kernel(q_ref, k_ref, v_ref, kc_ref, vc_ref, t_ref, o_ref, *, rows, kh):
    kctx = kc_ref[...]
    vctx = vc_ref[...]

    def body(r, carry):
        rs = jnp.clip(r - kh // 2, 0, rows - kh)
        qoff = pl.multiple_of(r * GRID_W, GRID_W)
        koff = pl.multiple_of(rs * GRID_W, GRID_W)
        q = q_ref[pl.ds(qoff, GRID_W), :]
        s_loc = _nt_dot(q, k_ref[pl.ds(koff, kh * GRID_W), :]) + t_ref[r - rs]
        s_ctx = _nt_dot(q, kctx)
        m = jnp.maximum(jnp.max(s_loc, axis=-1, keepdims=True), jnp.max(s_ctx, axis=-1, keepdims=True))
        p_loc = jnp.exp(s_loc - m)
        p_ctx = jnp.exp(s_ctx - m)
        l = jnp.sum(p_loc, axis=-1, keepdims=True) + jnp.sum(p_ctx, axis=-1, keepdims=True)
        o = (jnp.dot(p_loc.astype(BF16), v_ref[pl.ds(koff, kh * GRID_W), :], preferred_element_type=F32)
             + jnp.dot(p_ctx.astype(BF16), vctx, preferred_element_type=F32))
        o_ref[pl.ds(qoff, GRID_W), :] = (o / l).astype(o_ref.dtype)
        return carry

    lax.fori_loop(0, rows, body, 0)


def _neighbourhood(px, pc, table, *, n_lat, n_ctx, heads, unit_heads):
    bsz = px.shape[0] // n_lat
    rows = n_lat // GRID_W
    kh = min(NA_ROWS, rows)
    qc, kc, vc = UNIT_CQ * unit_heads, UNIT_CK * unit_heads, UNIT_CV * unit_heads
    lat = lambda c0: pl.BlockSpec((n_lat, HEAD_DIM), lambda b, h: (b, c0 + h))
    ctx = lambda c0: pl.BlockSpec((n_ctx, HEAD_DIM), lambda b, h: (b, c0 + h))
    return pl.pallas_call(
        functools.partial(_na_kernel, rows=rows, kh=kh),
        grid=(bsz, heads),
        in_specs=[lat(qc), lat(kc), lat(vc), ctx(kc), ctx(vc),
                  pl.BlockSpec((None, kh, GRID_W, kh * GRID_W), lambda b, h: (h, 0, 0, 0))],
        out_specs=pl.BlockSpec((n_lat, HEAD_DIM), lambda b, h: (b, h)),
        out_shape=jax.ShapeDtypeStruct((bsz * n_lat, heads * HEAD_DIM), BF16),
        compiler_params=_cparams(("parallel", "parallel"), 32),
    )(px, px, px, pc, pc, table)


def _outproj_kernel(oa_ref, ob_ref, oc_ref, w_ref, x_ref, g_ref, o_ref):
    wa, wb = oa_ref.shape[1], ob_ref.shape[1]
    acc = jnp.dot(oa_ref[...], w_ref[0:wa, :], preferred_element_type=F32)
    acc += jnp.dot(ob_ref[...], w_ref[wa:wa + wb, :], preferred_element_type=F32)
    acc += jnp.dot(oc_ref[...], w_ref[wa + wb:, :], preferred_element_type=F32)
    o_ref[...] = x_ref[...] + g_ref[...] * acc


def _out_proj(oa, ob, oc, w, x, gate, row_fn):
    bsz, n, d = x.shape
    tm = min(1024, n)
    tn = min(512, d)
    nt = n // tm
    lhs = lambda a: pl.BlockSpec((tm, a.shape[1]), lambda b, i, j: (b * nt + i, 0))
    return pl.pallas_call(
        _outproj_kernel,
        grid=(bsz, nt, d // tn),
        in_specs=[lhs(oa), lhs(ob), lhs(oc),
                  pl.BlockSpec((w.shape[0], tn), lambda b, i, j: (0, j)),
                  pl.BlockSpec((None, tm, tn), lambda b, i, j: (b, i, j)),
                  pl.BlockSpec((None, 1, tn), lambda b, i, j: (row_fn(b), 0, j))],
        out_specs=pl.BlockSpec((None, tm, tn), lambda b, i, j: (b, i, j)),
        out_shape=jax.ShapeDtypeStruct((bsz, n, d), F32),
        compiler_params=_cparams(("parallel", "parallel", "arbitrary"), 48),
    )(oa, ob, oc, w, x, gate)


def _split3(x):
    hi = x.astype(BF16)
    r = x - hi.astype(F32)
    mid = r.astype(BF16)
    lo = (r - mid.astype(F32)).astype(BF16)
    return hi, mid, lo


def _router_kernel(x_ref, g_ref, sh_ref, sc_ref, wh_ref, wl_ref, h_ref, afft_ref, aff_ref):
    y = _rms(x_ref[...], g_ref[...])
    h = y * (1.0 + sc_ref[...]) + sh_ref[...]
    h_ref[...] = h
    hh = h.astype(BF16)
    hl = (h - hh.astype(F32)).astype(BF16)
    wh, wl = wh_ref[...], wl_ref[...]
    lt = _nt_dot(wh, hh) + _nt_dot(wh, hl) + _nt_dot(wl, hh)
    e = jnp.exp(lt - jnp.max(lt, axis=0, keepdims=True))
    afft_ref[...] = e / jnp.sum(e, axis=0, keepdims=True)
    ln = _nt_dot(hh, wh) + _nt_dot(hl, wh) + _nt_dot(hh, wl)
    en = jnp.exp(ln - jnp.max(ln, axis=-1, keepdims=True))
    aff_ref[...] = en / jnp.sum(en, axis=-1, keepdims=True)


def _router(x, g, shift, scale, row_fn, w_router):
    bsz, n, d = x.shape
    ne = w_router.shape[1]
    tm = min(256, n)
    wt = w_router.T
    wh = wt.astype(BF16)
    wl = (wt - wh.astype(F32)).astype(BF16)
    wspec = pl.BlockSpec((ne, d), lambda b, i: (0, 0))
    return pl.pallas_call(
        _router_kernel,
        grid=(bsz, n // tm),
        in_specs=[pl.BlockSpec((None, tm, d), lambda b, i: (b, i, 0)),
                  pl.BlockSpec((1, d), lambda b, i: (0, 0)),
                  _row_spec(d, row_fn), _row_spec(d, row_fn), wspec, wspec],
        out_specs=[pl.BlockSpec((None, tm, d), lambda b, i: (b, i, 0)),
                   pl.BlockSpec((None, ne, tm), lambda b, i: (b, 0, i)),
                   pl.BlockSpec((None, tm, ne), lambda b, i: (b, i, 0))],
        out_shape=[jax.ShapeDtypeStruct((bsz, n, d), F32),
                   jax.ShapeDtypeStruct((bsz, ne, n), F32),
                   jax.ShapeDtypeStruct((bsz, n, ne), F32)],
        compiler_params=_cparams(("parallel", "parallel"), 40),
    )(x, g.reshape(1, d), shift, scale, wh, wl)


def _lane_cumsum(x01, tri):
    r, n = x01.shape
    carry = jnp.zeros((r, 1), F32)
    out = []
    for c in range(n // HEAD_DIM):
        blk = x01[:, c * HEAD_DIM:(c + 1) * HEAD_DIM].astype(BF16)
        cs = jnp.dot(blk, tri, preferred_element_type=F32) + carry
        out.append(cs)
        carry = cs[:, HEAD_DIM - 1:HEAD_DIM]
    return jnp.concatenate(out, axis=1) if len(out) > 1 else out[0]


def _select_kernel(afft_ref, r_ref, o_ref, *, cap):
    a = afft_ref[...]
    ne, n = a.shape
    bits = pltpu.bitcast(a, jnp.int32)

    def bisect(_, lohi):
        lo, hi = lohi
        mid = lo + lax.shift_right_logical(hi - lo, 1)
        cnt = jnp.sum((bits >= mid).astype(F32), axis=1, keepdims=True)
        ok = cnt >= cap
        return jnp.where(ok, mid, lo), jnp.where(ok, hi, mid)

    lo0 = jnp.zeros((ne, 1), jnp.int32)
    hi0 = jnp.full((ne, 1), 0x7F800000, jnp.int32)
    thr, _ = lax.fori_loop(0, 32, bisect, (lo0, hi0))

    ri = lax.broadcasted_iota(jnp.int32, (HEAD_DIM, HEAD_DIM), 0)
    ci = lax.broadcasted_iota(jnp.int32, (HEAD_DIM, HEAD_DIM), 1)
    tri = (ri <= ci).astype(BF16)
    gt = (bits > thr).astype(F32)
    eq = (bits == thr).astype(F32)
    need = cap - jnp.sum(gt, axis=1, keepdims=True)
    sel = gt + eq * (_lane_cumsum(eq, tri) <= need).astype(F32)
    pos = _lane_cumsum(sel, tri) - 1.0

    slot = lax.broadcasted_iota(jnp.int32, (cap, n), 0).astype(F32)
    rmat = r_ref[...]
    for e in range(ne):
        onehot = jnp.where((pos[e:e + 1, :] == slot) & (sel[e:e + 1, :] > 0.5), 1.0, 0.0).astype(BF16)
        o_ref[e] = jnp.dot(onehot, rmat, preferred_element_type=F32)


def _select(afft, aff, cap):
    bsz, ne, n = afft.shape
    t = jnp.arange(n)
    hi, mid, lo = _split3(aff)
    tcols = jnp.stack([t // 64, t % 64], axis=1).astype(BF16)
    pad = jnp.zeros((bsz, n, HEAD_DIM - 2 - 3 * ne), BF16)
    rmat = jnp.concatenate([jnp.broadcast_to(tcols[None], (bsz, n, 2)), hi, mid, lo, pad], axis=2)
    tab = pl.pallas_call(
        functools.partial(_select_kernel, cap=cap),
        grid=(bsz,),
        in_specs=[pl.BlockSpec((None, ne, n), lambda b: (b, 0, 0)),
                  pl.BlockSpec((None, n, HEAD_DIM), lambda b: (b, 0, 0))],
        out_specs=pl.BlockSpec((None, ne, cap, HEAD_DIM), lambda b: (b, 0, 0, 0)),
        out_shape=jax.ShapeDtypeStruct((bsz, ne, cap, HEAD_DIM), F32),
        compiler_params=_cparams(("parallel",), 40),
    )(afft, rmat)
    idx = (tab[..., 0] * 64.0 + tab[..., 1]).astype(jnp.int32)
    ev = jnp.arange(ne)[None, :, None, None]
    parts = [jnp.take_along_axis(tab, jnp.broadcast_to(2 + k * ne + ev, (bsz, ne, cap, 1)), axis=3) for k in range(3)]
    return idx, parts[0] + parts[1] + parts[2]


def _ffn_up_kernel(idx_ref, h_hbm, wg_ref, wu_ref, o_ref, xg_ref, sem, *, cap):
    e, b = pl.program_id(0), pl.program_id(1)
    base = (b * pl.num_programs(0) + e) * cap

    def row_copy(j, src_row):
        return pltpu.make_async_copy(h_hbm.at[b, pl.ds(src_row, 1), :], xg_ref.at[pl.ds(j, 1), :], sem)

    def issue(j, c):
        row_copy(j, idx_ref[base + j]).start()
        return c

    def drain(j, c):
        row_copy(j, 0).wait()
        return c

    lax.fori_loop(0, cap, issue, 0)
    lax.fori_loop(0, cap, drain, 0)
    x = xg_ref[...].astype(BF16)
    gate = jnp.dot(x, wg_ref[...], preferred_element_type=F32)
    up = jnp.dot(x, wu_ref[...], preferred_element_type=F32)
    o_ref[...] = (gate * jax.nn.sigmoid(gate) * up).astype(o_ref.dtype)


def _ffn_up(idx_flat, h, wg, wu, cap):
    bsz, n, d = h.shape
    ne, _, ff = wg.shape
    wspec = pl.BlockSpec((None, d, ff), lambda e, b, idx: (e, 0, 0))
    return pl.pallas_call(
        functools.partial(_ffn_up_kernel, cap=cap),
        grid_spec=pltpu.PrefetchScalarGridSpec(
            num_scalar_prefetch=1,
            grid=(ne, bsz),
            in_specs=[pl.BlockSpec(memory_space=pl.ANY), wspec, wspec],
            out_specs=pl.BlockSpec((None, None, cap, ff), lambda e, b, idx: (b, e, 0, 0)),
            scratch_shapes=[pltpu.VMEM((cap, d), F32), pltpu.SemaphoreType.DMA(())]),
        out_shape=jax.ShapeDtypeStruct((bsz, ne, cap, ff), BF16),
        compiler_params=_cparams(("arbitrary", "arbitrary"), 48),
    )(idx_flat, h, wg, wu)


def _ffn_down_kernel(idx_ref, hid_ref, wd_ref, gate_ref, x_ref, g2_ref, o_ref, ye_ref, *, cap):
    b, e = pl.program_id(0), pl.program_id(2)
    ne = pl.num_programs(2)
    base = (b * ne + e) * cap

    @pl.when(e == 0)
    def _():
        o_ref[...] = jnp.zeros_like(o_ref)

    ye_ref[...] = jnp.dot(hid_ref[...], wd_ref[...], preferred_element_type=F32) * gate_ref[...]

    def scatter(j, c):
        t = idx_ref[base + j]
        o_ref[pl.ds(t, 1), :] += ye_ref[pl.ds(j, 1), :]
        return c

    lax.fori_loop(0, cap, scatter, 0)

    @pl.when(e == ne - 1)
    def _():
        o_ref[...] = x_ref[...] + g2_ref[...] * o_ref[...]


def _ffn_down(idx_flat, hid, wd, gates, x, g2, row_fn):
    bsz, n, d = x.shape
    ne, ff, _ = wd.shape
    cap = hid.shape[2]
    cb = min(512, d)
    return pl.pallas_call(
        functools.partial(_ffn_down_kernel, cap=cap),
        grid_spec=pltpu.PrefetchScalarGridSpec(
            num_scalar_prefetch=1,
            grid=(bsz, d // cb, ne),
            in_specs=[pl.BlockSpec((None, None, cap, ff), lambda b, c, e, idx: (b, e, 0, 0)),
                      pl.BlockSpec((None, ff, cb), lambda b, c, e, idx: (e, 0, c)),
                      pl.BlockSpec((None, None, cap, 1), lambda b, c, e, idx: (b, e, 0, 0)),
                      pl.BlockSpec((None, n, cb), lambda b, c, e, idx: (b, 0, c)),
                      pl.BlockSpec((None, 1, cb), lambda b, c, e, idx: (row_fn(b), 0, c))],
            out_specs=pl.BlockSpec((None, n, cb), lambda b, c, e, idx: (b, 0, c)),
            scratch_shapes=[pltpu.VMEM((cap, cb), F32)]),
        out_shape=jax.ShapeDtypeStruct((bsz, n, d), F32),
        compiler_params=_cparams(("parallel", "parallel", "arbitrary"), 48),
    )(idx_flat, hid, wd, gates, x, g2)


def _expert_ffn_residual(x, norm_g, shift, scale, g2, row_fn, w_router, wg, wu, wd):
    n = x.shape[1]
    ne = w_router.shape[1]
    cap = EC_CAPACITY * n // ne
    h, afft, aff = _router(x, norm_g, shift, scale, row_fn, w_router)
    idx, gates = _select(afft, aff, cap)
    idx_flat = idx.reshape(-1)
    hid = _ffn_up(idx_flat, h, wg, wu, cap)
    return _ffn_down(idx_flat, hid, wd, gates, x, g2, row_fn)


def kernel(x, c, ctx, c_ctx, w_ada, b_ada, norm1_g, norm2_g, w_in, w_out, a_lambda, a_subln_g, b_q_norm_g,
           b_k_norm_g, c_rpb, w_router, w_e_gate, w_e_up, w_e_down, final_g):
    bsz, n, d = x.shape
    n_ctx = ctx.shape[1]
    depth = w_ada.shape[0]
    rows = n // GRID_W
    unit_heads = d // 8 // HEAD_DIM
    a_heads, b_heads, c_heads = 2 * unit_heads, 3 * unit_heads, 3 * unit_heads
    assert bsz + 1 <= MOD_ROWS and rows >= 1 and n % GRID_W == 0

    cvec = jnp.zeros((MOD_ROWS, d), F32).at[:bsz].set(c).at[bsz].set(c_ctx)
    mod = _ada_mod(cvec, w_ada, b_ada)
    lat_row = lambda b: b
    ctx_row = lambda b: bsz

    tabs_a, tabs_b = _rope_tables(n, "A", False), _rope_tables(n, "B", False)
    tabs_id = _rope_tables(n_ctx, "A", True)

    for l in range(depth):
        need_ctx = l < depth - 1
        lam_init = 0.8 - 0.6 * float(np.exp(-0.3 * l))
        m6 = [mod[l, :, k * d:(k + 1) * d].reshape(MOD_ROWS, 1, d) for k in range(N_MOD)]
        sh1, sc1, g1, sh2, sc2, g2 = m6
        w_in_l = w_in[l].astype(BF16)
        w_out_l = w_out[l].astype(BF16)
        wg, wu, wd = w_e_gate[l].astype(BF16), w_e_up[l].astype(BF16), w_e_down[l].astype(BF16)

        hx = _norm_mod(x, norm1_g[l], sh1, sc1, lat_row, BF16).reshape(bsz * n, d)
        hc = _norm_mod(ctx, norm1_g[l], sh1, sc1, ctx_row, BF16).reshape(bsz * n_ctx, d)
        px = _in_proj(hx, w_in_l, tabs_a, tabs_b, b_q_norm_g[l], b_k_norm_g[l], n)
        pc = _in_proj(hc, w_in_l, tabs_id, tabs_id, b_q_norm_g[l], b_k_norm_g[l], n_ctx)

        common = dict(n_ctx=n_ctx, n_lat=n, unit_heads=unit_heads)
        a_args = dict(heads=a_heads, kv_group=1, q_unit=UNIT_AQ, k_unit=UNIT_AK, v_unit=UNIT_AV, diff=True,
                      lam=a_lambda[l], subln_g=a_subln_g[l], lam_init=lam_init)
        b_args = dict(heads=b_heads, kv_group=b_heads // unit_heads, q_unit=UNIT_BQ, k_unit=UNIT_BK, v_unit=UNIT_BV)
        oa = _flash(px, pc, px, n_q=n, **a_args, **common)
        ob = _flash(px, pc, px, n_q=n, **b_args, **common)
        oc = _neighbourhood(px, pc, _na_bias_table(c_rpb[l], rows), n_lat=n, n_ctx=n_ctx, heads=c_heads,
                            unit_heads=unit_heads)
        x = _out_proj(oa, ob, oc, w_out_l, x, g1, lat_row)
        x = _expert_ffn_residual(x, norm2_g[l], sh2, sc2, g2, lat_row, w_router[l], wg, wu, wd)

        if need_ctx:
            c_args = dict(heads=c_heads, kv_group=1, q_unit=UNIT_CQ, k_unit=UNIT_CK, v_unit=UNIT_CV)
            oa_c = _flash(pc, pc, None, n_q=n_ctx, **a_args, **common)
            ob_c = _flash(pc, pc, None, n_q=n_ctx, **b_args, **common)
            oc_c = _flash(pc, pc, None, n_q=n_ctx, **c_args, **common)
            ctx = _out_proj(oa_c, ob_c, oc_c, w_out_l, ctx, g1, ctx_row)
            ctx = _expert_ffn_residual(ctx, norm2_g[l], sh2, sc2, g2, ctx_row, w_router[l], wg, wu, wd)

    return _final_norm(x, final_g)
```

```python
import functools

import numpy as np
import jax
import jax.numpy as jnp
from jax import lax
from jax.experimental import pallas as pl
from jax.experimental.pallas import tpu as pltpu

HEAD_DIM = 128
A_QK_DIM = HEAD_DIM // 2
GRID_W = 64
NA_ROWS = 8
NA_COLS = 16
EC_CAPACITY = 2
N_MOD = 6
ROPE_BASE = 10000.0
EPS = 1e-6
NEG = -1e30
LOG2E = 1.4426950408889634
A_Q_SCALE = A_QK_DIM ** -0.5 * LOG2E
Q_SCALE = HEAD_DIM ** -0.5 * LOG2E
MIB = 1024 * 1024
MOD_ROWS = 8
F32 = jnp.float32
BF16 = jnp.bfloat16

UNIT_AQ, UNIT_AK, UNIT_AV = 0, 2, 4
UNIT_BQ, UNIT_BK, UNIT_BV = 6, 9, 10
UNIT_CQ, UNIT_CK, UNIT_CV = 11, 14, 17
N_UNITS = 20


def _cparams(sem, vmem_mib):
    return pltpu.CompilerParams(dimension_semantics=sem, vmem_limit_bytes=vmem_mib * MIB)


def _nt_dot(a, b):
    return lax.dot_general(a, b, (((1,), (1,)), ((), ())), preferred_element_type=F32)


def _ada_kernel(c_ref, w_ref, b_ref, o_ref):
    c = c_ref[...]
    s = (c * jax.nn.sigmoid(c)).astype(BF16)
    o_ref[...] = jnp.dot(s, w_ref[...].astype(BF16), preferred_element_type=F32) + b_ref[...]


def _ada_mod(cvec, w_ada, b_ada):
    depth, d, nm = w_ada.shape
    tn = min(512, nm)
    return pl.pallas_call(
        _ada_kernel,
        grid=(depth, nm // tn),
        in_specs=[pl.BlockSpec((MOD_ROWS, d), lambda l, j: (0, 0)),
                  pl.BlockSpec((None, d, tn), lambda l, j: (l, 0, j)),
                  pl.BlockSpec((None, 1, tn), lambda l, j: (l, 0, j))],
        out_specs=pl.BlockSpec((None, MOD_ROWS, tn), lambda l, j: (l, 0, j)),
        out_shape=jax.ShapeDtypeStruct((depth, MOD_ROWS, nm), F32),
        compiler_params=_cparams(("parallel", "parallel"), 40),
    )(cvec, w_ada, b_ada.reshape(depth, 1, nm))


def _rms(x, g):
    return x * lax.rsqrt(jnp.mean(x * x, axis=-1, keepdims=True) + EPS) * g


def _norm_mod_kernel(x_ref, g_ref, sh_ref, sc_ref, o_ref):
    y = _rms(x_ref[...], g_ref[...])
    o_ref[...] = (y * (1.0 + sc_ref[...]) + sh_ref[...]).astype(o_ref.dtype)


def _norm_kernel(x_ref, g_ref, o_ref):
    o_ref[...] = _rms(x_ref[...], g_ref[...]).astype(o_ref.dtype)


def _row_spec(d, row_fn):
    return pl.BlockSpec((None, 1, d), lambda b, i: (row_fn(b), 0, 0))


def _norm_mod(x, g, shift, scale, row_fn, out_dtype):
    bsz, n, d = x.shape
    tm = min(256, n)
    return pl.pallas_call(
        _norm_mod_kernel,
        grid=(bsz, n // tm),
        in_specs=[pl.BlockSpec((None, tm, d), lambda b, i: (b, i, 0)),
                  pl.BlockSpec((1, d), lambda b, i: (0, 0)),
                  _row_spec(d, row_fn), _row_spec(d, row_fn)],
        out_specs=pl.BlockSpec((None, tm, d), lambda b, i: (b, i, 0)),
        out_shape=jax.ShapeDtypeStruct((bsz, n, d), out_dtype),
        compiler_params=_cparams(("parallel", "parallel"), 32),
    )(x, g.reshape(1, d), shift, scale)


def _final_norm(x, g):
    bsz, n, d = x.shape
    tm = min(256, n)
    return pl.pallas_call(
        _norm_kernel,
        grid=(bsz, n // tm),
        in_specs=[pl.BlockSpec((None, tm, d), lambda b, i: (b, i, 0)),
                  pl.BlockSpec((1, d), lambda b, i: (0, 0))],
        out_specs=pl.BlockSpec((None, tm, d), lambda b, i: (b, i, 0)),
        out_shape=jax.ShapeDtypeStruct((bsz, n, d), x.dtype),
        compiler_params=_cparams(("parallel", "parallel"), 32),
    )(x, g.reshape(1, d))


def _rope_tables(n, group, identity):
    if identity:
        return jnp.ones((n, HEAD_DIM), F32), jnp.zeros((n, HEAD_DIM), F32)
    p = jnp.arange(HEAD_DIM)
    half = 16 if group == "A" else 32
    use_col = ((p // (2 * half)) % 2) == 1
    first = (p % (2 * half)) < half
    freqs = ROPE_BASE ** (-(p % half).astype(F32) / half)
    t = jnp.arange(n)
    pos = jnp.where(use_col[None, :], (t % GRID_W)[:, None], (t // GRID_W)[:, None]).astype(F32)
    ang = pos * freqs[None, :]
    return jnp.cos(ang), jnp.where(first[None, :], -jnp.sin(ang), jnp.sin(ang))


def _rotate(x, cos, sin_signed, half):
    lane = lax.broadcasted_iota(jnp.int32, x.shape, 1)
    first = (lane % (2 * half)) < half
    partner = jnp.where(first, pltpu.roll(x, HEAD_DIM - half, 1), pltpu.roll(x, half, 1))
    return x * cos + partner * sin_signed


def _inproj_kernel(h_ref, w_ref, ca_ref, sa_ref, cb_ref, sb_ref, gq_ref, gk_ref, o_ref):
    j = pl.program_id(1)
    acc = jnp.dot(h_ref[...], w_ref[...], preferred_element_type=F32)
    nh = acc.shape[1] // HEAD_DIM
    heads = [slice(i * HEAD_DIM, (i + 1) * HEAD_DIM) for i in range(nh)]

    @pl.when(j < UNIT_AV)
    def _():
        scale = jnp.where(j < UNIT_AK, A_Q_SCALE, 1.0).astype(F32)
        for s in heads:
            o_ref[:, s] = (_rotate(acc[:, s], ca_ref[...], sa_ref[...], 16) * scale).astype(o_ref.dtype)

    @pl.when((j >= UNIT_BQ) & (j < UNIT_BV))
    def _():
        is_q = j < UNIT_BK
        g = jnp.where(is_q, gq_ref[...], gk_ref[...])
        scale = jnp.where(is_q, Q_SCALE, 1.0).astype(F32)
        for s in heads:
            y = _rms(acc[:, s], g)
            o_ref[:, s] = (_rotate(y, cb_ref[...], sb_ref[...], 32) * scale).astype(o_ref.dtype)

    @pl.when((j >= UNIT_CQ) & (j < UNIT_CK))
    def _():
        o_ref[...] = (acc * Q_SCALE).astype(o_ref.dtype)

    @pl.when(((j >= UNIT_AV) & (j < UNIT_BQ)) | (j == UNIT_BV) | (j >= UNIT_CK))
    def _():
        o_ref[...] = acc.astype(o_ref.dtype)


def _in_proj(h, w, tabs_a, tabs_b, gq, gk, n_per_sample):
    m, d = h.shape
    wid = w.shape[1]
    tn = wid // N_UNITS
    tm = min(1024, n_per_sample)
    tiles = n_per_sample // tm
    tab = pl.BlockSpec((tm, HEAD_DIM), lambda i, j: (i % tiles, 0))
    vec = pl.BlockSpec((1, HEAD_DIM), lambda i, j: (0, 0))
    return pl.pallas_call(
        _inproj_kernel,
        grid=(m // tm, N_UNITS),
        in_specs=[pl.BlockSpec((tm, d), lambda i, j: (i, 0)),
                  pl.BlockSpec((d, tn), lambda i, j: (0, j)),
                  tab, tab, tab, tab, vec, vec],
        out_specs=pl.BlockSpec((tm, tn), lambda i, j: (i, j)),
        out_shape=jax.ShapeDtypeStruct((m, wid), BF16),
        compiler_params=_cparams(("parallel", "arbitrary"), 48),
    )(h, w, tabs_a[0], tabs_a[1], tabs_b[0], tabs_b[1], gq.reshape(1, HEAD_DIM), gk.reshape(1, HEAD_DIM))


def _softmax_pv(q, kc, vc, kl, vl):
    s_c = _nt_dot(q, kc)
    m = jnp.max(s_c, axis=-1, keepdims=True)
    if kl is not None:
        s_l = _nt_dot(q, kl)
        m = jnp.maximum(m, jnp.max(s_l, axis=-1, keepdims=True))
    p_c = jnp.exp2(s_c - m)
    l = jnp.sum(p_c, axis=-1, keepdims=True)
    o = jnp.dot(p_c.astype(BF16), vc, preferred_element_type=F32)
    if kl is not None:
        p_l = jnp.exp2(s_l - m)
        l = l + jnp.sum(p_l, axis=-1, keepdims=True)
        o = o + jnp.dot(p_l.astype(BF16), vl, preferred_element_type=F32)
    return o, l


def _attn_kernel(*refs, diff, has_lat, group, lam_init):
    it = iter(refs)
    q_ref, kc_ref, vc_ref = next(it), next(it), next(it)
    kl = vl = lam_ref = g_ref = None
    if has_lat:
        kl, vl = next(it)[...], next(it)[...]
    if diff:
        lam_ref, g_ref = next(it), next(it)
    o_ref = next(it)
    kc, vc = kc_ref[...], vc_ref[...]

    for g in range(group):
        cols = slice(g * HEAD_DIM, (g + 1) * HEAD_DIM)
        q = q_ref[:, cols]
        if diff:
            qf = q.astype(F32)
            lane = lax.broadcasted_iota(jnp.int32, qf.shape, 1)
            o1, l1 = _softmax_pv(jnp.where(lane < A_QK_DIM, qf, 0.0).astype(BF16), kc, vc, kl, vl)
            o2, l2 = _softmax_pv(jnp.where(lane >= A_QK_DIM, qf, 0.0).astype(BF16), kc, vc, kl, vl)
            lv = lam_ref[...]
            lam = (jnp.exp(jnp.sum(lv[0:1] * lv[1:2], axis=-1, keepdims=True))
                   - jnp.exp(jnp.sum(lv[2:3] * lv[3:4], axis=-1, keepdims=True)) + lam_init)
            o = o1 * (1.0 / l1) - (lam * (1.0 / l2)) * o2
            o = _rms(o, g_ref[...]) * (1.0 - lam_init)
        else:
            o1, l1 = _softmax_pv(q, kc, vc, kl, vl)
            o = o1 * (1.0 / l1)
        o_ref[:, cols] = o.astype(o_ref.dtype)


def _attention(pq, pc, px, *, n_q, n_ctx, n_lat, kv_heads, group, q_unit, k_unit, v_unit, unit_heads,
               diff=False, lam=None, subln_g=None, lam_init=0.0):
    bsz = pq.shape[0] // n_q
    has_lat = px is not None
    tq = min(256, n_q)
    nqt = n_q // tq
    gw = group * HEAD_DIM
    qc, kc, vc = q_unit * unit_heads // group, k_unit * unit_heads, v_unit * unit_heads
    assert (q_unit * unit_heads) % group == 0
    in_specs = [pl.BlockSpec((tq, gw), lambda b, h, i: (b * nqt + i, qc + h)),
                pl.BlockSpec((n_ctx, HEAD_DIM), lambda b, h, i: (b, kc + h)),
                pl.BlockSpec((n_ctx, HEAD_DIM), lambda b, h, i: (b, vc + h))]
    args = [pq, pc, pc]
    if has_lat:
        in_specs += [pl.BlockSpec((n_lat, HEAD_DIM), lambda b, h, i: (b, kc + h)),
                     pl.BlockSpec((n_lat, HEAD_DIM), lambda b, h, i: (b, vc + h))]
        args += [px, px]
    if diff:
        in_specs += [pl.BlockSpec((4, A_QK_DIM), lambda b, h, i: (0, 0)),
                     pl.BlockSpec((1, HEAD_DIM), lambda b, h, i: (0, 0))]
        args += [lam, subln_g.reshape(1, HEAD_DIM)]
    return pl.pallas_call(
        functools.partial(_attn_kernel, diff=diff, has_lat=has_lat, group=group, lam_init=lam_init),
        grid=(bsz, kv_heads, nqt),
        in_specs=in_specs,
        out_specs=pl.BlockSpec((tq, gw), lambda b, h, i: (b * nqt + i, h)),
        out_shape=jax.ShapeDtypeStruct((bsz * n_q, kv_heads * gw), BF16),
        compiler_params=_cparams(("parallel", "parallel", "arbitrary"), 48),
    )(*args)


def _na_bias_tables(c_rpb, rows):
    nl, nh = c_rpb.shape[:2]
    kh = min(NA_ROWS, rows)
    q = np.arange(GRID_W)
    kc = np.arange(GRID_W)
    start = np.clip(q - NA_COLS // 2, 0, GRID_W - NA_COLS)
    inwin = (kc[None, :] >= start[:, None]) & (kc[None, :] < start[:, None] + NA_COLS)
    ci = kc[None, :] - q[:, None] + NA_COLS - 1
    onehot = ((ci[..., None] == np.arange(2 * NA_COLS - 1)) & inwin[..., None]).astype(np.float32)
    toep = jnp.einsum("lhab,qcb->lhaqc", c_rpb.astype(F32), jnp.asarray(onehot), precision=lax.Precision.HIGHEST)
    toep = jnp.where(jnp.asarray(inwin)[None, None, None], toep * LOG2E, NEG)
    t = jnp.stack([toep[:, :, NA_ROWS - 1 - v:NA_ROWS - 1 - v + kh] for v in range(kh)], axis=2)
    return t.transpose(0, 1, 2, 4, 3, 5).reshape(nl, nh, kh, GRID_W, kh * GRID_W)


def _na_kernel(q_ref, k_ref, v_ref, kc_ref, vc_ref, t_ref, o_ref, *, rows, kh):
    kctx = kc_ref[...]
    vctx = vc_ref[...]

    def body(r, carry):
        rs = jnp.clip(r - kh // 2, 0, rows - kh)
        qoff = pl.multiple_of(r * GRID_W, GRID_W)
        koff = pl.multiple_of(rs * GRID_W, GRID_W)
        q = q_ref[pl.ds(qoff, GRID_W), :]
        s_loc = _nt_dot(q, k_ref[pl.ds(koff, kh * GRID_W), :]) + t_ref[r - rs]
        s_ctx = _nt_dot(q, kctx)
        m = jnp.maximum(jnp.max(s_loc, axis=-1, keepdims=True), jnp.max(s_ctx, axis=-1, keepdims=True))
        p_loc = jnp.exp2(s_loc - m)
        p_ctx = jnp.exp2(s_ctx - m)
        l = jnp.sum(p_loc, axis=-1, keepdims=True) + jnp.sum(p_ctx, axis=-1, keepdims=True)
        o = (jnp.dot(p_loc.astype(BF16), v_ref[pl.ds(koff, kh * GRID_W), :], preferred_element_type=F32)
             + jnp.dot(p_ctx.astype(BF16), vctx, preferred_element_type=F32))
        o_ref[pl.ds(qoff, GRID_W), :] = (o * (1.0 / l)).astype(o_ref.dtype)
        return carry

    lax.fori_loop(0, rows, body, 0, unroll=4)


def _neighbourhood(px, pc, table, *, n_lat, n_ctx, heads, unit_heads):
    bsz = px.shape[0] // n_lat
    rows = n_lat // GRID_W
    kh = min(NA_ROWS, rows)
    qc, kc, vc = UNIT_CQ * unit_heads, UNIT_CK * unit_heads, UNIT_CV * unit_heads
    lat = lambda c0: pl.BlockSpec((n_lat, HEAD_DIM), lambda b, h: (b, c0 + h))
    ctx = lambda c0: pl.BlockSpec((n_ctx, HEAD_DIM), lambda b, h: (b, c0 + h))
    return pl.pallas_call(
        functools.partial(_na_kernel, rows=rows, kh=kh),
        grid=(bsz, heads),
        in_specs=[lat(qc), lat(kc), lat(vc), ctx(kc), ctx(vc),
                  pl.BlockSpec((None, kh, GRID_W, kh * GRID_W), lambda b, h: (h, 0, 0, 0))],
        out_specs=pl.BlockSpec((n_lat, HEAD_DIM), lambda b, h: (b, h)),
        out_shape=jax.ShapeDtypeStruct((bsz * n_lat, heads * HEAD_DIM), BF16),
        compiler_params=_cparams(("parallel", "parallel"), 32),
    )(px, px, px, pc, pc, table)


def _outproj_kernel(oa_ref, ob_ref, oc_ref, w_ref, x_ref, g_ref, o_ref):
    wa, wb = oa_ref.shape[1], ob_ref.shape[1]
    acc = jnp.dot(oa_ref[...], w_ref[0:wa, :], preferred_element_type=F32)
    acc += jnp.dot(ob_ref[...], w_ref[wa:wa + wb, :], preferred_element_type=F32)
    acc += jnp.dot(oc_ref[...], w_ref[wa + wb:, :], preferred_element_type=F32)
    o_ref[...] = x_ref[...] + g_ref[...] * acc


def _out_proj(oa, ob, oc, w, x, gate, row_fn):
    bsz, n, d = x.shape
    tm = min(1024, n)
    tn = min(512, d)
    nt = n // tm
    lhs = lambda a: pl.BlockSpec((tm, a.shape[1]), lambda b, i, j: (b * nt + i, 0))
    return pl.pallas_call(
        _outproj_kernel,
        grid=(bsz, nt, d // tn),
        in_specs=[lhs(oa), lhs(ob), lhs(oc),
                  pl.BlockSpec((w.shape[0], tn), lambda b, i, j: (0, j)),
                  pl.BlockSpec((None, tm, tn), lambda b, i, j: (b, i, j)),
                  pl.BlockSpec((None, 1, tn), lambda b, i, j: (row_fn(b), 0, j))],
        out_specs=pl.BlockSpec((None, tm, tn), lambda b, i, j: (b, i, j)),
        out_shape=jax.ShapeDtypeStruct((bsz, n, d), F32),
        compiler_params=_cparams(("parallel", "parallel", "arbitrary"), 48),
    )(oa, ob, oc, w, x, gate)


def _split3(x):
    hi = x.astype(BF16)
    r = x - hi.astype(F32)
    mid = r.astype(BF16)
    lo = (r - mid.astype(F32)).astype(BF16)
    return hi, mid, lo


def _router_kernel(x_ref, g_ref, sh_ref, sc_ref, wh_ref, wl_ref, h_ref, afft_ref, aff_ref):
    y = _rms(x_ref[...], g_ref[...])
    h = y * (1.0 + sc_ref[...]) + sh_ref[...]
    h_ref[...] = h
    hh = h.astype(BF16)
    hl = (h - hh.astype(F32)).astype(BF16)
    wh, wl = wh_ref[...], wl_ref[...]
    lt = _nt_dot(wh, hh) + _nt_dot(wh, hl) + _nt_dot(wl, hh)
    e = jnp.exp(lt - jnp.max(lt, axis=0, keepdims=True))
    afft_ref[...] = e / jnp.sum(e, axis=0, keepdims=True)
    ln = _nt_dot(hh, wh) + _nt_dot(hl, wh) + _nt_dot(hh, wl)
    en = jnp.exp(ln - jnp.max(ln, axis=-1, keepdims=True))
    aff_ref[...] = en / jnp.sum(en, axis=-1, keepdims=True)


def _router(x, g, shift, scale, row_fn, w_router):
    bsz, n, d = x.shape
    ne = w_router.shape[1]
    tm = min(256, n)
    wt = w_router.T
    wh = wt.astype(BF16)
    wl = (wt - wh.astype(F32)).astype(BF16)
    wspec = pl.BlockSpec((ne, d), lambda b, i: (0, 0))
    return pl.pallas_call(
        _router_kernel,
        grid=(bsz, n // tm),
        in_specs=[pl.BlockSpec((None, tm, d), lambda b, i: (b, i, 0)),
                  pl.BlockSpec((1, d), lambda b, i: (0, 0)),
                  _row_spec(d, row_fn), _row_spec(d, row_fn), wspec, wspec],
        out_specs=[pl.BlockSpec((None, tm, d), lambda b, i: (b, i, 0)),
                   pl.BlockSpec((None, ne, tm), lambda b, i: (b, 0, i)),
                   pl.BlockSpec((None, tm, ne), lambda b, i: (b, i, 0))],
        out_shape=[jax.ShapeDtypeStruct((bsz, n, d), F32),
                   jax.ShapeDtypeStruct((bsz, ne, n), F32),
                   jax.ShapeDtypeStruct((bsz, n, ne), F32)],
        compiler_params=_cparams(("parallel", "parallel"), 40),
    )(x, g.reshape(1, d), shift, scale, wh, wl)


def _lane_cumsum(x01, tri):
    r, n = x01.shape
    carry = jnp.zeros((r, 1), F32)
    out = []
    for c in range(n // HEAD_DIM):
        blk = x01[:, c * HEAD_DIM:(c + 1) * HEAD_DIM].astype(BF16)
        cs = jnp.dot(blk, tri, preferred_element_type=F32) + carry
        out.append(cs)
        carry = cs[:, HEAD_DIM - 1:HEAD_DIM]
    return jnp.concatenate(out, axis=1) if len(out) > 1 else out[0]


def _select_kernel(afft_ref, r_ref, o_ref, *, cap):
    a = afft_ref[...]
    ne, n = a.shape
    bits = pltpu.bitcast(a, jnp.int32)

    def bisect(_, lohi):
        lo, hi = lohi
        mid = lo + lax.shift_right_logical(hi - lo, 1)
        cnt = jnp.sum((bits >= mid).astype(F32), axis=1, keepdims=True)
        ok = cnt >= cap
        return jnp.where(ok, mid, lo), jnp.where(ok, hi, mid)

    lo0 = jnp.zeros((ne, 1), jnp.int32)
    hi0 = jnp.full((ne, 1), 0x7F800000, jnp.int32)
    thr, _ = lax.fori_loop(0, 32, bisect, (lo0, hi0))

    ri = lax.broadcasted_iota(jnp.int32, (HEAD_DIM, HEAD_DIM), 0)
    ci = lax.broadcasted_iota(jnp.int32, (HEAD_DIM, HEAD_DIM), 1)
    tri = (ri <= ci).astype(BF16)
    gt = (bits > thr).astype(F32)
    eq = (bits == thr).astype(F32)
    need = cap - jnp.sum(gt, axis=1, keepdims=True)
    sel = gt + eq * (_lane_cumsum(eq, tri) <= need).astype(F32)
    pos = _lane_cumsum(sel, tri) - 1.0

    slot = lax.broadcasted_iota(jnp.int32, (cap, n), 0).astype(F32)
    rmat = r_ref[...]
    for e in range(ne):
        onehot = jnp.where((pos[e:e + 1, :] == slot) & (sel[e:e + 1, :] > 0.5), 1.0, 0.0).astype(BF16)
        o_ref[e] = jnp.dot(onehot, rmat, preferred_element_type=F32)


def _select(afft, aff, cap):
    bsz, ne, n = afft.shape
    t = jnp.arange(n)
    hi, mid, lo = _split3(aff)
    tcols = jnp.stack([t // 64, t % 64], axis=1).astype(BF16)
    pad = jnp.zeros((bsz, n, HEAD_DIM - 2 - 3 * ne), BF16)
    rmat = jnp.concatenate([jnp.broadcast_to(tcols[None], (bsz, n, 2)), hi, mid, lo, pad], axis=2)
    tab = pl.pallas_call(
        functools.partial(_select_kernel, cap=cap),
        grid=(bsz,),
        in_specs=[pl.BlockSpec((None, ne, n), lambda b: (b, 0, 0)),
                  pl.BlockSpec((None, n, HEAD_DIM), lambda b: (b, 0, 0))],
        out_specs=pl.BlockSpec((None, ne, cap, HEAD_DIM), lambda b: (b, 0, 0, 0)),
        out_shape=jax.ShapeDtypeStruct((bsz, ne, cap, HEAD_DIM), F32),
        compiler_params=_cparams(("parallel",), 40),
    )(afft, rmat)
    idx = (tab[..., 0] * 64.0 + tab[..., 1]).astype(jnp.int32)
    parts = tab[..., 2:2 + 3 * ne].reshape(bsz, ne, cap, 3, ne).sum(axis=3)
    gates = (parts * jnp.eye(ne, dtype=F32)[None, :, None, :]).sum(axis=-1, keepdims=True)
    return idx, gates


def _ffn_up_kernel(idx_ref, h_hbm, wg_ref, wu_ref, o_ref, xg_ref, sem, *, cap, ne, nb):
    e, b = pl.program_id(0), pl.program_id(1)
    step = e * nb + b
    slot = lax.rem(step, 2)

    def gather(e_, b_, slot_):
        base = (b_ * ne + e_) * cap

        def issue(j, c):
            pltpu.make_async_copy(h_hbm.at[b_, pl.ds(idx_ref[base + j], 1), :],
                                  xg_ref.at[slot_, pl.ds(j, 1), :], sem.at[slot_]).start()
            return c

        lax.fori_loop(0, cap, issue, 0, unroll=8)

    @pl.when(step == 0)
    def _():
        gather(e, b, 0)

    pltpu.make_async_copy(h_hbm.at[b, pl.ds(0, cap), :], xg_ref.at[slot], sem.at[slot]).wait()

    @pl.when(step + 1 < ne * nb)
    def _():
        nxt = step + 1
        gather(lax.div(nxt, nb), lax.rem(nxt, nb), 1 - slot)

    x = xg_ref[slot].astype(BF16)
    gate = jnp.dot(x, wg_ref[...], preferred_element_type=F32)
    up = jnp.dot(x, wu_ref[...], preferred_element_type=F32)
    o_ref[...] = (gate * jax.nn.sigmoid(gate) * up).astype(o_ref.dtype)


def _ffn_up(idx_flat, h, wg, wu, cap):
    bsz, n, d = h.shape
    ne, _, ff = wg.shape
    wspec = pl.BlockSpec((None, d, ff), lambda e, b, idx: (e, 0, 0))
    return pl.pallas_call(
        functools.partial(_ffn_up_kernel, cap=cap, ne=ne, nb=bsz),
        grid_spec=pltpu.PrefetchScalarGridSpec(
            num_scalar_prefetch=1,
            grid=(ne, bsz),
            in_specs=[pl.BlockSpec(memory_space=pl.ANY), wspec, wspec],
            out_specs=pl.BlockSpec((None, None, cap, ff), lambda e, b, idx: (b, e, 0, 0)),
            scratch_shapes=[pltpu.VMEM((2, cap, d), F32), pltpu.SemaphoreType.DMA((2,))]),
        out_shape=jax.ShapeDtypeStruct((bsz, ne, cap, ff), BF16),
        compiler_params=_cparams(("arbitrary", "arbitrary"), 48),
    )(idx_flat, h, wg, wu)


SCATTER_ROWS = 4


def _ffn_down_kernel(idx_ref, hid_ref, wd_ref, gate_ref, x_ref, g2_ref, o_ref, acc_ref, ye_ref, *, cap, ne, tr):
    b, e = pl.program_id(0), pl.program_id(2)

    @pl.when(e == 0)
    def _():
        acc_ref[...] = jnp.zeros_like(acc_ref)

    @pl.when(e < ne)
    def _():
        ye_ref[...] = jnp.dot(hid_ref[...], wd_ref[...], preferred_element_type=F32) * gate_ref[...]
        base = (b * ne + e) * cap

        def group(gi, c):
            j0 = gi * SCATTER_ROWS
            toks = [idx_ref[base + j0 + u] for u in range(SCATTER_ROWS)]
            rows = [acc_ref[pl.ds(t, 1), :] for t in toks]
            for u, (t, r) in enumerate(zip(toks, rows)):
                acc_ref[pl.ds(t, 1), :] = r + ye_ref[pl.ds(j0 + u, 1), :]
            return c

        lax.fori_loop(0, cap // SCATTER_ROWS, group, 0, unroll=2)

    @pl.when(e >= ne)
    def _():
        r0 = pl.multiple_of((e - ne) * tr, tr)
        o_ref[...] = x_ref[...] + g2_ref[...] * acc_ref[pl.ds(r0, tr), :]


def _ffn_down(idx_flat, hid, wd, gates, x, g2, row_fn):
    bsz, n, d = x.shape
    ne, ff, _ = wd.shape
    cap = hid.shape[2]
    assert cap % SCATTER_ROWS == 0
    cb = min(1024, d)
    tr = min(512, n)
    ex = lambda e: jnp.minimum(e, ne - 1)
    rt = lambda e: jnp.maximum(e - ne, 0)
    return pl.pallas_call(
        functools.partial(_ffn_down_kernel, cap=cap, ne=ne, tr=tr),
        grid_spec=pltpu.PrefetchScalarGridSpec(
            num_scalar_prefetch=1,
            grid=(bsz, d // cb, ne + n // tr),
            in_specs=[pl.BlockSpec((None, None, cap, ff), lambda b, c, e, idx: (b, ex(e), 0, 0)),
                      pl.BlockSpec((None, ff, cb), lambda b, c, e, idx: (ex(e), 0, c)),
                      pl.BlockSpec((None, None, cap, 1), lambda b, c, e, idx: (b, ex(e), 0, 0)),
                      pl.BlockSpec((None, tr, cb), lambda b, c, e, idx: (b, rt(e), c)),
                      pl.BlockSpec((None, 1, cb), lambda b, c, e, idx: (row_fn(b), 0, c))],
            out_specs=pl.BlockSpec((None, tr, cb), lambda b, c, e, idx: (b, rt(e), c)),
            scratch_shapes=[pltpu.VMEM((n, cb), F32), pltpu.VMEM((cap, cb), F32)]),
        out_shape=jax.ShapeDtypeStruct((bsz, n, d), F32),
        compiler_params=_cparams(("parallel", "parallel", "arbitrary"), 48),
    )(idx_flat, hid, wd, gates, x, g2)


def _expert_ffn_residual(x, norm_g, shift, scale, g2, row_fn, w_router, wg, wu, wd):
    n = x.shape[1]
    ne = w_router.shape[1]
    cap = EC_CAPACITY * n // ne
    h, afft, aff = _router(x, norm_g, shift, scale, row_fn, w_router)
    idx, gates = _select(afft, aff, cap)
    idx_flat = idx.reshape(-1)
    hid = _ffn_up(idx_flat, h, wg, wu, cap)
    return _ffn_down(idx_flat, hid, wd, gates, x, g2, row_fn)


def kernel(x, c, ctx, c_ctx, w_ada, b_ada, norm1_g, norm2_g, w_in, w_out, a_lambda, a_subln_g, b_q_norm_g,
           b_k_norm_g, c_rpb, w_router, w_e_gate, w_e_up, w_e_down, final_g):
    bsz, n, d = x.shape
    n_ctx = ctx.shape[1]
    depth = w_ada.shape[0]
    rows = n // GRID_W
    unit_heads = d // 8 // HEAD_DIM
    a_heads, b_heads, c_heads = 2 * unit_heads, 3 * unit_heads, 3 * unit_heads
    assert bsz + 1 <= MOD_ROWS and rows >= 1 and n % GRID_W == 0

    cvec = jnp.zeros((MOD_ROWS, d), F32).at[:bsz].set(c).at[bsz].set(c_ctx)
    mod = _ada_mod(cvec, w_ada, b_ada)
    lat_row = lambda b: b
    ctx_row = lambda b: bsz

    tabs_a, tabs_b = _rope_tables(n, "A", False), _rope_tables(n, "B", False)
    tabs_id = _rope_tables(n_ctx, "A", True)
    na_tables = _na_bias_tables(c_rpb, rows)

    for l in range(depth):
        need_ctx = l < depth - 1
        lam_init = 0.8 - 0.6 * float(np.exp(-0.3 * l))
        m6 = [mod[l, :, k * d:(k + 1) * d].reshape(MOD_ROWS, 1, d) for k in range(N_MOD)]
        sh1, sc1, g1, sh2, sc2, g2 = m6
        w_in_l = w_in[l].astype(BF16)
        w_out_l = w_out[l].astype(BF16)
        wg, wu, wd = w_e_gate[l].astype(BF16), w_e_up[l].astype(BF16), w_e_down[l].astype(BF16)

        hx = _norm_mod(x, norm1_g[l], sh1, sc1, lat_row, BF16).reshape(bsz * n, d)
        hc = _norm_mod(ctx, norm1_g[l], sh1, sc1, ctx_row, BF16).reshape(bsz * n_ctx, d)
        px = _in_proj(hx, w_in_l, tabs_a, tabs_b, b_q_norm_g[l], b_k_norm_g[l], n)
        pc = _in_proj(hc, w_in_l, tabs_id, tabs_id, b_q_norm_g[l], b_k_norm_g[l], n_ctx)

        common = dict(n_ctx=n_ctx, n_lat=n, unit_heads=unit_heads)
        a_args = dict(kv_heads=a_heads, group=1, q_unit=UNIT_AQ, k_unit=UNIT_AK, v_unit=UNIT_AV, diff=True,
                      lam=a_lambda[l], subln_g=a_subln_g[l], lam_init=lam_init)
        b_args = dict(kv_heads=unit_heads, group=b_heads // unit_heads, q_unit=UNIT_BQ, k_unit=UNIT_BK, v_unit=UNIT_BV)
        oa = _attention(px, pc, px, n_q=n, **a_args, **common)
        ob = _attention(px, pc, px, n_q=n, **b_args, **common)
        oc = _neighbourhood(px, pc, na_tables[l], n_lat=n, n_ctx=n_ctx, heads=c_heads, unit_heads=unit_heads)
        x = _out_proj(oa, ob, oc, w_out_l, x, g1, lat_row)
        x = _expert_ffn_residual(x, norm2_g[l], sh2, sc2, g2, lat_row, w_router[l], wg, wu, wd)

        if need_ctx:
            c_args = dict(kv_heads=c_heads, group=1, q_unit=UNIT_CQ, k_unit=UNIT_CK, v_unit=UNIT_CV)
            oa_c = _attention(pc, pc, None, n_q=n_ctx, **a_args, **common)
            ob_c = _attention(pc, pc, None, n_q=n_ctx, **b_args, **common)
            oc_c = _attention(pc, pc, None, n_q=n_ctx, **c_args, **common)
            ctx = _out_proj(oa_c, ob_c, oc_c, w_out_l, ctx, g1, ctx_row)
            ctx = _expert_ffn_residual(ctx, norm2_g[l], sh2, sc2, g2, ctx_row, w_router[l], wg, wu, wd)

    return _final_norm(x, final_g)
```

```python
import functools

import numpy as np
import jax
import jax.numpy as jnp
from jax import lax
from jax.experimental import pallas as pl
from jax.experimental.pallas import tpu as pltpu

HEAD_DIM = 128
A_QK_DIM = HEAD_DIM // 2
GRID_W = 64
NA_ROWS = 8
NA_COLS = 16
EC_CAPACITY = 2
N_MOD = 6
ROPE_BASE = 10000.0
EPS = 1e-6
NEG = -1e30
LOG2E = 1.4426950408889634
A_Q_SCALE = A_QK_DIM ** -0.5 * LOG2E
Q_SCALE = HEAD_DIM ** -0.5 * LOG2E
MIB = 1024 * 1024
MOD_ROWS = 8
F32 = jnp.float32
BF16 = jnp.bfloat16

UNIT_AQ, UNIT_AK, UNIT_AV = 0, 2, 4
UNIT_BQ, UNIT_BK, UNIT_BV = 6, 9, 10
UNIT_CQ, UNIT_CK, UNIT_CV = 11, 14, 17
N_UNITS = 20


def _cparams(sem, vmem_mib):
    return pltpu.CompilerParams(dimension_semantics=sem, vmem_limit_bytes=vmem_mib * MIB)


def _nt_dot(a, b):
    return lax.dot_general(a, b, (((1,), (1,)), ((), ())), preferred_element_type=F32)


def _ada_kernel(c_ref, w_ref, b_ref, o_ref):
    c = c_ref[...]
    s = (c * jax.nn.sigmoid(c)).astype(BF16)
    o_ref[...] = jnp.dot(s, w_ref[...].astype(BF16), preferred_element_type=F32) + b_ref[...]


def _ada_mod(cvec, w_ada, b_ada):
    depth, d, nm = w_ada.shape
    tn = min(512, nm)
    return pl.pallas_call(
        _ada_kernel,
        grid=(depth, nm // tn),
        in_specs=[pl.BlockSpec((MOD_ROWS, d), lambda l, j: (0, 0)),
                  pl.BlockSpec((None, d, tn), lambda l, j: (l, 0, j)),
                  pl.BlockSpec((None, 1, tn), lambda l, j: (l, 0, j))],
        out_specs=pl.BlockSpec((None, MOD_ROWS, tn), lambda l, j: (l, 0, j)),
        out_shape=jax.ShapeDtypeStruct((depth, MOD_ROWS, nm), F32),
        compiler_params=_cparams(("parallel", "parallel"), 40),
    )(cvec, w_ada, b_ada.reshape(depth, 1, nm))


def _rms(x, g):
    return x * lax.rsqrt(jnp.mean(x * x, axis=-1, keepdims=True) + EPS) * g


def _norm_mod_kernel(x_ref, g_ref, sh_ref, sc_ref, o_ref):
    y = _rms(x_ref[...], g_ref[...])
    o_ref[...] = (y * (1.0 + sc_ref[...]) + sh_ref[...]).astype(o_ref.dtype)


def _norm_kernel(x_ref, g_ref, o_ref):
    o_ref[...] = _rms(x_ref[...], g_ref[...]).astype(o_ref.dtype)


def _row_spec(d, row_fn):
    return pl.BlockSpec((None, 1, d), lambda b, i: (row_fn(b), 0, 0))


def _norm_mod(x, g, shift, scale, row_fn, out_dtype):
    bsz, n, d = x.shape
    tm = min(256, n)
    return pl.pallas_call(
        _norm_mod_kernel,
        grid=(bsz, n // tm),
        in_specs=[pl.BlockSpec((None, tm, d), lambda b, i: (b, i, 0)),
                  pl.BlockSpec((1, d), lambda b, i: (0, 0)),
                  _row_spec(d, row_fn), _row_spec(d, row_fn)],
        out_specs=pl.BlockSpec((None, tm, d), lambda b, i: (b, i, 0)),
        out_shape=jax.ShapeDtypeStruct((bsz, n, d), out_dtype),
        compiler_params=_cparams(("parallel", "parallel"), 32),
    )(x, g.reshape(1, d), shift, scale)


def _final_norm(x, g):
    bsz, n, d = x.shape
    tm = min(256, n)
    return pl.pallas_call(
        _norm_kernel,
        grid=(bsz, n // tm),
        in_specs=[pl.BlockSpec((None, tm, d), lambda b, i: (b, i, 0)),
                  pl.BlockSpec((1, d), lambda b, i: (0, 0))],
        out_specs=pl.BlockSpec((None, tm, d), lambda b, i: (b, i, 0)),
        out_shape=jax.ShapeDtypeStruct((bsz, n, d), x.dtype),
        compiler_params=_cparams(("parallel", "parallel"), 32),
    )(x, g.reshape(1, d))


def _rope_tables(n, group, identity):
    if identity:
        return jnp.ones((n, HEAD_DIM), F32), jnp.zeros((n, HEAD_DIM), F32)
    p = jnp.arange(HEAD_DIM)
    half = 16 if group == "A" else 32
    use_col = ((p // (2 * half)) % 2) == 1
    first = (p % (2 * half)) < half
    freqs = ROPE_BASE ** (-(p % half).astype(F32) / half)
    t = jnp.arange(n)
    pos = jnp.where(use_col[None, :], (t % GRID_W)[:, None], (t // GRID_W)[:, None]).astype(F32)
    ang = pos * freqs[None, :]
    return jnp.cos(ang), jnp.where(first[None, :], -jnp.sin(ang), jnp.sin(ang))


def _rotate(x, cos, sin_signed, half):
    lane = lax.broadcasted_iota(jnp.int32, x.shape, 1)
    first = (lane % (2 * half)) < half
    partner = jnp.where(first, pltpu.roll(x, HEAD_DIM - half, 1), pltpu.roll(x, half, 1))
    return x * cos + partner * sin_signed


def _inproj_kernel(h_ref, w_ref, ca_ref, sa_ref, cb_ref, sb_ref, gq_ref, gk_ref, o_ref):
    j = pl.program_id(1)
    acc = jnp.dot(h_ref[...], w_ref[...], preferred_element_type=F32)
    nh = acc.shape[1] // HEAD_DIM
    heads = [slice(i * HEAD_DIM, (i + 1) * HEAD_DIM) for i in range(nh)]

    @pl.when(j < UNIT_AV)
    def _():
        scale = jnp.where(j < UNIT_AK, A_Q_SCALE, 1.0).astype(F32)
        for s in heads:
            o_ref[:, s] = (_rotate(acc[:, s], ca_ref[...], sa_ref[...], 16) * scale).astype(o_ref.dtype)

    @pl.when((j >= UNIT_BQ) & (j < UNIT_BV))
    def _():
        is_q = j < UNIT_BK
        g = jnp.where(is_q, gq_ref[...], gk_ref[...])
        scale = jnp.where(is_q, Q_SCALE, 1.0).astype(F32)
        for s in heads:
            y = _rms(acc[:, s], g)
            o_ref[:, s] = (_rotate(y, cb_ref[...], sb_ref[...], 32) * scale).astype(o_ref.dtype)

    @pl.when((j >= UNIT_CQ) & (j < UNIT_CK))
    def _():
        o_ref[...] = (acc * Q_SCALE).astype(o_ref.dtype)

    @pl.when(((j >= UNIT_AV) & (j < UNIT_BQ)) | (j == UNIT_BV) | (j >= UNIT_CK))
    def _():
        o_ref[...] = acc.astype(o_ref.dtype)


def _in_proj(h, w, tabs_a, tabs_b, gq, gk, n_per_sample):
    m, d = h.shape
    wid = w.shape[1]
    tn = wid // N_UNITS
    tm = min(1024, n_per_sample)
    tiles = n_per_sample // tm
    tab = pl.BlockSpec((tm, HEAD_DIM), lambda i, j: (i % tiles, 0))
    vec = pl.BlockSpec((1, HEAD_DIM), lambda i, j: (0, 0))
    return pl.pallas_call(
        _inproj_kernel,
        grid=(m // tm, N_UNITS),
        in_specs=[pl.BlockSpec((tm, d), lambda i, j: (i, 0)),
                  pl.BlockSpec((d, tn), lambda i, j: (0, j)),
                  tab, tab, tab, tab, vec, vec],
        out_specs=pl.BlockSpec((tm, tn), lambda i, j: (i, j)),
        out_shape=jax.ShapeDtypeStruct((m, wid), BF16),
        compiler_params=_cparams(("parallel", "arbitrary"), 48),
    )(h, w, tabs_a[0], tabs_a[1], tabs_b[0], tabs_b[1], gq.reshape(1, HEAD_DIM), gk.reshape(1, HEAD_DIM))


def _probs(q, kc, kl):
    s_c = _nt_dot(q, kc)
    m = jnp.max(s_c, axis=-1, keepdims=True)
    if kl is None:
        p_c = jnp.exp2(s_c - m)
        return p_c, None, jnp.sum(p_c, axis=-1, keepdims=True)
    s_l = _nt_dot(q, kl)
    m = jnp.maximum(m, jnp.max(s_l, axis=-1, keepdims=True))
    p_c = jnp.exp2(s_c - m)
    p_l = jnp.exp2(s_l - m)
    return p_c, p_l, jnp.sum(p_c, axis=-1, keepdims=True) + jnp.sum(p_l, axis=-1, keepdims=True)


def _pv(p_c, p_l, vc, vl):
    o = jnp.dot(p_c.astype(BF16), vc, preferred_element_type=F32)
    if p_l is not None:
        o = o + jnp.dot(p_l.astype(BF16), vl, preferred_element_type=F32)
    return o


def _attn_kernel(*refs, diff, has_lat, group, lam_init):
    it = iter(refs)
    q_ref, kc_ref, vc_ref = next(it), next(it), next(it)
    kl = vl = lam_ref = g_ref = None
    if has_lat:
        kl, vl = next(it)[...], next(it)[...]
    if diff:
        lam_ref, g_ref = next(it), next(it)
    o_ref = next(it)
    kc, vc = kc_ref[...], vc_ref[...]

    for g in range(group):
        cols = slice(g * HEAD_DIM, (g + 1) * HEAD_DIM)
        q = q_ref[:, cols]
        if diff:
            qf = q.astype(F32)
            lane = lax.broadcasted_iota(jnp.int32, qf.shape, 1)
            p1c, p1l, l1 = _probs(jnp.where(lane < A_QK_DIM, qf, 0.0).astype(BF16), kc, kl)
            p2c, p2l, l2 = _probs(jnp.where(lane >= A_QK_DIM, qf, 0.0).astype(BF16), kc, kl)
            lv = lam_ref[...]
            lam = (jnp.exp(jnp.sum(lv[0:1] * lv[1:2], axis=-1, keepdims=True))
                   - jnp.exp(jnp.sum(lv[2:3] * lv[3:4], axis=-1, keepdims=True)) + lam_init)
            a1, a2 = 1.0 / l1, lam * (1.0 / l2)
            o = _pv(p1c * a1 - p2c * a2, None if p1l is None else p1l * a1 - p2l * a2, vc, vl)
            o = _rms(o, g_ref[...]) * (1.0 - lam_init)
        else:
            p_c, p_l, l1 = _probs(q, kc, kl)
            o = _pv(p_c, p_l, vc, vl) * (1.0 / l1)
        o_ref[:, cols] = o.astype(o_ref.dtype)


def _attention(pq, pc, px, *, n_q, n_ctx, n_lat, kv_heads, group, q_unit, k_unit, v_unit, unit_heads,
               diff=False, lam=None, subln_g=None, lam_init=0.0):
    bsz = pq.shape[0] // n_q
    has_lat = px is not None
    tq = min(256, n_q)
    nqt = n_q // tq
    gw = group * HEAD_DIM
    qc, kc, vc = q_unit * unit_heads // group, k_unit * unit_heads, v_unit * unit_heads
    assert (q_unit * unit_heads) % group == 0
    in_specs = [pl.BlockSpec((tq, gw), lambda b, h, i: (b * nqt + i, qc + h)),
                pl.BlockSpec((n_ctx, HEAD_DIM), lambda b, h, i: (b, kc + h)),
                pl.BlockSpec((n_ctx, HEAD_DIM), lambda b, h, i: (b, vc + h))]
    args = [pq, pc, pc]
    if has_lat:
        in_specs += [pl.BlockSpec((n_lat, HEAD_DIM), lambda b, h, i: (b, kc + h)),
                     pl.BlockSpec((n_lat, HEAD_DIM), lambda b, h, i: (b, vc + h))]
        args += [px, px]
    if diff:
        in_specs += [pl.BlockSpec((4, A_QK_DIM), lambda b, h, i: (0, 0)),
                     pl.BlockSpec((1, HEAD_DIM), lambda b, h, i: (0, 0))]
        args += [lam, subln_g.reshape(1, HEAD_DIM)]
    return pl.pallas_call(
        functools.partial(_attn_kernel, diff=diff, has_lat=has_lat, group=group, lam_init=lam_init),
        grid=(bsz, kv_heads, nqt),
        in_specs=in_specs,
        out_specs=pl.BlockSpec((tq, gw), lambda b, h, i: (b * nqt + i, h)),
        out_shape=jax.ShapeDtypeStruct((bsz * n_q, kv_heads * gw), BF16),
        compiler_params=_cparams(("parallel", "parallel", "arbitrary"), 48),
    )(*args)


NA_QROWS = 4


def _na_plan(rows):
    kh = min(NA_ROWS, rows)
    qb = min(NA_QROWS, rows)
    assert rows % qb == 0
    kw = min(rows, kh + qb)
    starts, var_ids, variants = [], [], []
    for r0 in range(0, rows, qb):
        rs = [int(np.clip(r - kh // 2, 0, rows - kh)) for r in range(r0, r0 + qb)]
        kstart = min(min(rs), rows - kw)
        assert kstart >= 0 and max(rs) + kh <= kstart + kw
        sig = tuple((r0 + u - kstart, rs[u] - kstart) for u in range(qb))
        if sig not in variants:
            variants.append(sig)
        starts.append(kstart)
        var_ids.append(variants.index(sig))
    return kw, starts, var_ids, variants


def _na_bias_tables(c_rpb, rows):
    nl, nh = c_rpb.shape[:2]
    kh = min(NA_ROWS, rows)
    kw, _, _, variants = _na_plan(rows)
    q = np.arange(GRID_W)
    kc = np.arange(GRID_W)
    start = np.clip(q - NA_COLS // 2, 0, GRID_W - NA_COLS)
    inwin = (kc[None, :] >= start[:, None]) & (kc[None, :] < start[:, None] + NA_COLS)
    ci = kc[None, :] - q[:, None] + NA_COLS - 1
    onehot = ((ci[..., None] == np.arange(2 * NA_COLS - 1)) & inwin[..., None]).astype(np.float32)
    toep = jnp.einsum("lhab,qcb->lhaqc", c_rpb.astype(F32), jnp.asarray(onehot), precision=lax.Precision.HIGHEST)
    toep = jnp.where(jnp.asarray(inwin)[None, None, None], toep * LOG2E, NEG)
    masked = jnp.full((nl, nh, GRID_W, GRID_W), NEG, F32)
    tables = []
    for sig in variants:
        per_row = []
        for qoff, woff in sig:
            blocks = [toep[:, :, kr - qoff + NA_ROWS - 1] if woff <= kr < woff + kh else masked for kr in range(kw)]
            per_row.append(jnp.concatenate(blocks, axis=-1))
        tables.append(jnp.concatenate(per_row, axis=2))
    return jnp.stack(tables, axis=2)


def _na_kernel(kst_ref, var_ref, q_ref, k_ref, v_ref, kc_ref, vc_ref, t_ref, o_ref, *, nblk, qlen, klen):
    kctx = kc_ref[...]
    vctx = vc_ref[...]

    def body(rb, carry):
        qoff = pl.multiple_of(rb * qlen, qlen)
        koff = pl.multiple_of(kst_ref[rb] * GRID_W, GRID_W)
        q = q_ref[pl.ds(qoff, qlen), :]
        s_loc = _nt_dot(q, k_ref[pl.ds(koff, klen), :]) + t_ref[var_ref[rb]]
        s_ctx = _nt_dot(q, kctx)
        m = jnp.maximum(jnp.max(s_loc, axis=-1, keepdims=True), jnp.max(s_ctx, axis=-1, keepdims=True))
        p_loc = jnp.exp2(s_loc - m)
        p_ctx = jnp.exp2(s_ctx - m)
        l = jnp.sum(p_loc, axis=-1, keepdims=True) + jnp.sum(p_ctx, axis=-1, keepdims=True)
        o = (jnp.dot(p_loc.astype(BF16), v_ref[pl.ds(koff, klen), :], preferred_element_type=F32)
             + jnp.dot(p_ctx.astype(BF16), vctx, preferred_element_type=F32))
        o_ref[pl.ds(qoff, qlen), :] = (o * (1.0 / l)).astype(o_ref.dtype)
        return carry

    lax.fori_loop(0, nblk, body, 0, unroll=2 if nblk % 2 == 0 else 1)


def _neighbourhood(px, pc, table, *, n_lat, n_ctx, heads, unit_heads):
    bsz = px.shape[0] // n_lat
    rows = n_lat // GRID_W
    kw, starts, var_ids, variants = _na_plan(rows)
    nblk = len(starts)
    qlen, klen = n_lat // nblk, kw * GRID_W
    qc, kc, vc = UNIT_CQ * unit_heads, UNIT_CK * unit_heads, UNIT_CV * unit_heads
    lat = lambda c0: pl.BlockSpec((n_lat, HEAD_DIM), lambda b, h, ks, vi: (b, c0 + h))
    ctx = lambda c0: pl.BlockSpec((n_ctx, HEAD_DIM), lambda b, h, ks, vi: (b, c0 + h))
    return pl.pallas_call(
        functools.partial(_na_kernel, nblk=nblk, qlen=qlen, klen=klen),
        grid_spec=pltpu.PrefetchScalarGridSpec(
            num_scalar_prefetch=2,
            grid=(bsz, heads),
            in_specs=[lat(qc), lat(kc), lat(vc), ctx(kc), ctx(vc),
                      pl.BlockSpec((None, len(variants), qlen, klen), lambda b, h, ks, vi: (h, 0, 0, 0))],
            out_specs=pl.BlockSpec((n_lat, HEAD_DIM), lambda b, h, ks, vi: (b, h))),
        out_shape=jax.ShapeDtypeStruct((bsz * n_lat, heads * HEAD_DIM), BF16),
        compiler_params=_cparams(("parallel", "parallel"), 40),
    )(jnp.asarray(starts, jnp.int32), jnp.asarray(var_ids, jnp.int32), px, px, px, pc, pc, table)


def _outproj_kernel(oa_ref, ob_ref, oc_ref, w_ref, x_ref, g_ref, o_ref):
    wa, wb = oa_ref.shape[1], ob_ref.shape[1]
    acc = jnp.dot(oa_ref[...], w_ref[0:wa, :], preferred_element_type=F32)
    acc += jnp.dot(ob_ref[...], w_ref[wa:wa + wb, :], preferred_element_type=F32)
    acc += jnp.dot(oc_ref[...], w_ref[wa + wb:, :], preferred_element_type=F32)
    o_ref[...] = x_ref[...] + g_ref[...] * acc


def _out_proj(oa, ob, oc, w, x, gate, row_fn):
    bsz, n, d = x.shape
    tm = min(1024, n)
    tn = min(512, d)
    nt = n // tm
    lhs = lambda a: pl.BlockSpec((tm, a.shape[1]), lambda b, i, j: (b * nt + i, 0))
    return pl.pallas_call(
        _outproj_kernel,
        grid=(bsz, nt, d // tn),
        in_specs=[lhs(oa), lhs(ob), lhs(oc),
                  pl.BlockSpec((w.shape[0], tn), lambda b, i, j: (0, j)),
                  pl.BlockSpec((None, tm, tn), lambda b, i, j: (b, i, j)),
                  pl.BlockSpec((None, 1, tn), lambda b, i, j: (row_fn(b), 0, j))],
        out_specs=pl.BlockSpec((None, tm, tn), lambda b, i, j: (b, i, j)),
        out_shape=jax.ShapeDtypeStruct((bsz, n, d), F32),
        compiler_params=_cparams(("parallel", "parallel", "arbitrary"), 48),
    )(oa, ob, oc, w, x, gate)


def _split3(x):
    hi = x.astype(BF16)
    r = x - hi.astype(F32)
    mid = r.astype(BF16)
    lo = (r - mid.astype(F32)).astype(BF16)
    return hi, mid, lo


def _router_kernel(x_ref, g_ref, sh_ref, sc_ref, wh_ref, wl_ref, h_ref, afft_ref, aff_ref):
    y = _rms(x_ref[...], g_ref[...])
    h = y * (1.0 + sc_ref[...]) + sh_ref[...]
    h_ref[...] = h
    hh = h.astype(BF16)
    hl = (h - hh.astype(F32)).astype(BF16)
    wh, wl = wh_ref[...], wl_ref[...]
    lt = _nt_dot(wh, hh) + _nt_dot(wh, hl) + _nt_dot(wl, hh)
    e = jnp.exp(lt - jnp.max(lt, axis=0, keepdims=True))
    afft_ref[...] = e / jnp.sum(e, axis=0, keepdims=True)
    ln = _nt_dot(hh, wh) + _nt_dot(hl, wh) + _nt_dot(hh, wl)
    en = jnp.exp(ln - jnp.max(ln, axis=-1, keepdims=True))
    aff_ref[...] = en / jnp.sum(en, axis=-1, keepdims=True)


def _router(x, g, shift, scale, row_fn, w_router):
    bsz, n, d = x.shape
    ne = w_router.shape[1]
    tm = min(256, n)
    wt = w_router.T
    wh = wt.astype(BF16)
    wl = (wt - wh.astype(F32)).astype(BF16)
    wspec = pl.BlockSpec((ne, d), lambda b, i: (0, 0))
    return pl.pallas_call(
        _router_kernel,
        grid=(bsz, n // tm),
        in_specs=[pl.BlockSpec((None, tm, d), lambda b, i: (b, i, 0)),
                  pl.BlockSpec((1, d), lambda b, i: (0, 0)),
                  _row_spec(d, row_fn), _row_spec(d, row_fn), wspec, wspec],
        out_specs=[pl.BlockSpec((None, tm, d), lambda b, i: (b, i, 0)),
                   pl.BlockSpec((None, ne, tm), lambda b, i: (b, 0, i)),
                   pl.BlockSpec((None, tm, ne), lambda b, i: (b, i, 0))],
        out_shape=[jax.ShapeDtypeStruct((bsz, n, d), F32),
                   jax.ShapeDtypeStruct((bsz, ne, n), F32),
                   jax.ShapeDtypeStruct((bsz, n, ne), F32)],
        compiler_params=_cparams(("parallel", "parallel"), 40),
    )(x, g.reshape(1, d), shift, scale, wh, wl)


def _lane_cumsum(x01, tri):
    r, n = x01.shape
    carry = jnp.zeros((r, 1), F32)
    out = []
    for c in range(n // HEAD_DIM):
        blk = x01[:, c * HEAD_DIM:(c + 1) * HEAD_DIM].astype(BF16)
        cs = jnp.dot(blk, tri, preferred_element_type=F32) + carry
        out.append(cs)
        carry = cs[:, HEAD_DIM - 1:HEAD_DIM]
    return jnp.concatenate(out, axis=1) if len(out) > 1 else out[0]


def _select_kernel(afft_ref, r_ref, o_ref, *, cap):
    a = afft_ref[...]
    ne, n = a.shape
    bits = pltpu.bitcast(a, jnp.int32)

    def bisect(_, lohi):
        lo, hi = lohi
        mid = lo + lax.shift_right_logical(hi - lo, 1)
        cnt = jnp.sum((bits >= mid).astype(F32), axis=1, keepdims=True)
        ok = cnt >= cap
        return jnp.where(ok, mid, lo), jnp.where(ok, hi, mid)

    lo0 = jnp.zeros((ne, 1), jnp.int32)
    hi0 = jnp.full((ne, 1), 0x7F800000, jnp.int32)
    thr, _ = lax.fori_loop(0, 32, bisect, (lo0, hi0))

    ri = lax.broadcasted_iota(jnp.int32, (HEAD_DIM, HEAD_DIM), 0)
    ci = lax.broadcasted_iota(jnp.int32, (HEAD_DIM, HEAD_DIM), 1)
    tri = (ri <= ci).astype(BF16)
    gt = (bits > thr).astype(F32)
    eq = (bits == thr).astype(F32)
    need = cap - jnp.sum(gt, axis=1, keepdims=True)
    sel = gt + eq * (_lane_cumsum(eq, tri) <= need).astype(F32)
    pos = _lane_cumsum(sel, tri) - 1.0

    slot = lax.broadcasted_iota(jnp.int32, (cap, n), 0).astype(F32)
    rmat = r_ref[...]
    for e in range(ne):
        onehot = jnp.where((pos[e:e + 1, :] == slot) & (sel[e:e + 1, :] > 0.5), 1.0, 0.0).astype(BF16)
        o_ref[e] = jnp.dot(onehot, rmat, preferred_element_type=F32)


def _select(afft, aff, cap):
    bsz, ne, n = afft.shape
    t = jnp.arange(n)
    hi, mid, lo = _split3(aff)
    tcols = jnp.stack([t // 64, t % 64], axis=1).astype(BF16)
    pad = jnp.zeros((bsz, n, HEAD_DIM - 2 - 3 * ne), BF16)
    rmat = jnp.concatenate([jnp.broadcast_to(tcols[None], (bsz, n, 2)), hi, mid, lo, pad], axis=2)
    tab = pl.pallas_call(
        functools.partial(_select_kernel, cap=cap),
        grid=(bsz,),
        in_specs=[pl.BlockSpec((None, ne, n), lambda b: (b, 0, 0)),
                  pl.BlockSpec((None, n, HEAD_DIM), lambda b: (b, 0, 0))],
        out_specs=pl.BlockSpec((None, ne, cap, HEAD_DIM), lambda b: (b, 0, 0, 0)),
        out_shape=jax.ShapeDtypeStruct((bsz, ne, cap, HEAD_DIM), F32),
        compiler_params=_cparams(("parallel",), 40),
    )(afft, rmat)
    idx = (tab[..., 0] * 64.0 + tab[..., 1]).astype(jnp.int32)
    parts = tab[..., 2:2 + 3 * ne].reshape(bsz, ne, cap, 3, ne).sum(axis=3)
    gates = (parts * jnp.eye(ne, dtype=F32)[None, :, None, :]).sum(axis=-1, keepdims=True)
    return idx, gates


def _ffn_up_kernel(idx_ref, h_hbm, wg_ref, wu_ref, o_ref, xg_ref, sem, *, cap, ne, nb):
    e, b = pl.program_id(0), pl.program_id(1)
    step = e * nb + b
    slot = lax.rem(step, 2)

    def gather(e_, b_, slot_):
        base = (b_ * ne + e_) * cap
        for j in range(cap):
            pltpu.make_async_copy(h_hbm.at[b_, pl.ds(idx_ref[base + j], 1), :],
                                  xg_ref.at[slot_, pl.ds(j, 1), :], sem.at[slot_]).start()

    @pl.when(step == 0)
    def _():
        gather(e, b, 0)

    pltpu.make_async_copy(h_hbm.at[b, pl.ds(0, cap), :], xg_ref.at[slot], sem.at[slot]).wait()

    last = step + 1 >= ne * nb
    nxt = jnp.where(last, step, step + 1)
    gather(lax.div(nxt, nb), lax.rem(nxt, nb), 1 - slot)

    x = xg_ref[slot].astype(BF16)
    gate = jnp.dot(x, wg_ref[...].astype(BF16), preferred_element_type=F32)
    up = jnp.dot(x, wu_ref[...].astype(BF16), preferred_element_type=F32)
    o_ref[...] = (gate * jax.nn.sigmoid(gate) * up).astype(o_ref.dtype)

    @pl.when(last)
    def _():
        pltpu.make_async_copy(h_hbm.at[b, pl.ds(0, cap), :], xg_ref.at[1 - slot], sem.at[1 - slot]).wait()


def _ffn_up(idx_flat, h, wg, wu, cap):
    bsz, n, d = h.shape
    ne, _, ff = wg.shape
    wspec = pl.BlockSpec((None, d, ff), lambda e, b, idx: (e, 0, 0))
    return pl.pallas_call(
        functools.partial(_ffn_up_kernel, cap=cap, ne=ne, nb=bsz),
        grid_spec=pltpu.PrefetchScalarGridSpec(
            num_scalar_prefetch=1,
            grid=(ne, bsz),
            in_specs=[pl.BlockSpec(memory_space=pl.ANY), wspec, wspec],
            out_specs=pl.BlockSpec((None, None, cap, ff), lambda e, b, idx: (b, e, 0, 0)),
            scratch_shapes=[pltpu.VMEM((2, cap, d), F32), pltpu.SemaphoreType.DMA((2,))]),
        out_shape=jax.ShapeDtypeStruct((bsz, ne, cap, ff), BF16),
        compiler_params=_cparams(("arbitrary", "arbitrary"), 58),
    )(idx_flat, h, wg, wu)


ACC_FULL_WIDTH_BYTES = 8 * MIB
SCATTER_ROWS = 4


def _ffn_down_kernel(idx_ref, hid_ref, wd_ref, gate_ref, x_ref, g2_ref, o_ref, acc_ref, ye_ref, *, cap, ne, tr):
    b, e = pl.program_id(0), pl.program_id(2)

    @pl.when(e == 0)
    def _():
        acc_ref[...] = jnp.zeros_like(acc_ref)

    @pl.when(e < ne)
    def _():
        ye_ref[...] = jnp.dot(hid_ref[...], wd_ref[...].astype(BF16), preferred_element_type=F32) * gate_ref[...]
        base = (b * ne + e) * cap

        def group(gi, c):
            j0 = gi * SCATTER_ROWS
            toks = [idx_ref[base + j0 + u] for u in range(SCATTER_ROWS)]
            rows = [acc_ref[pl.ds(t, 1), :] for t in toks]
            for u, (t, r) in enumerate(zip(toks, rows)):
                acc_ref[pl.ds(t, 1), :] = r + ye_ref[pl.ds(j0 + u, 1), :]
            return c

        lax.fori_loop(0, cap // SCATTER_ROWS, group, 0, unroll=2)

    @pl.when(e >= ne)
    def _():
        r0 = pl.multiple_of((e - ne) * tr, tr)
        o_ref[...] = x_ref[...] + g2_ref[...] * acc_ref[pl.ds(r0, tr), :]


def _ffn_down(idx_flat, hid, wd, gates, x, g2, row_fn):
    bsz, n, d = x.shape
    ne, ff, _ = wd.shape
    cap = hid.shape[2]
    assert cap % SCATTER_ROWS == 0
    cb = d if n * d * 4 <= ACC_FULL_WIDTH_BYTES else min(1024, d)
    tr = min(512, n)
    ex = lambda e: jnp.minimum(e, ne - 1)
    rt = lambda e: jnp.maximum(e - ne, 0)
    return pl.pallas_call(
        functools.partial(_ffn_down_kernel, cap=cap, ne=ne, tr=tr),
        grid_spec=pltpu.PrefetchScalarGridSpec(
            num_scalar_prefetch=1,
            grid=(bsz, d // cb, ne + n // tr),
            in_specs=[pl.BlockSpec((None, None, cap, ff), lambda b, c, e, idx: (b, ex(e), 0, 0)),
                      pl.BlockSpec((None, ff, cb), lambda b, c, e, idx: (ex(e), 0, c)),
                      pl.BlockSpec((None, None, cap, 1), lambda b, c, e, idx: (b, ex(e), 0, 0)),
                      pl.BlockSpec((None, tr, cb), lambda b, c, e, idx: (b, rt(e), c)),
                      pl.BlockSpec((None, 1, cb), lambda b, c, e, idx: (row_fn(b), 0, c))],
            out_specs=pl.BlockSpec((None, tr, cb), lambda b, c, e, idx: (b, rt(e), c)),
            scratch_shapes=[pltpu.VMEM((n, cb), F32), pltpu.VMEM((cap, cb), F32)]),
        out_shape=jax.ShapeDtypeStruct((bsz, n, d), F32),
        compiler_params=_cparams(("parallel", "parallel", "arbitrary"), 48),
    )(idx_flat, hid, wd, gates, x, g2)


def _expert_ffn_residual(x, norm_g, shift, scale, g2, row_fn, w_router, wg, wu, wd):
    n = x.shape[1]
    ne = w_router.shape[1]
    cap = EC_CAPACITY * n // ne
    h, afft, aff = _router(x, norm_g, shift, scale, row_fn, w_router)
    idx, gates = _select(afft, aff, cap)
    idx_flat = idx.reshape(-1)
    hid = _ffn_up(idx_flat, h, wg, wu, cap)
    return _ffn_down(idx_flat, hid, wd, gates, x, g2, row_fn)


def kernel(x, c, ctx, c_ctx, w_ada, b_ada, norm1_g, norm2_g, w_in, w_out, a_lambda, a_subln_g, b_q_norm_g,
           b_k_norm_g, c_rpb, w_router, w_e_gate, w_e_up, w_e_down, final_g):
    bsz, n, d = x.shape
    n_ctx = ctx.shape[1]
    depth = w_ada.shape[0]
    rows = n // GRID_W
    unit_heads = d // 8 // HEAD_DIM
    a_heads, b_heads, c_heads = 2 * unit_heads, 3 * unit_heads, 3 * unit_heads
    assert bsz + 1 <= MOD_ROWS and rows >= 1 and n % GRID_W == 0

    cvec = jnp.zeros((MOD_ROWS, d), F32).at[:bsz].set(c).at[bsz].set(c_ctx)
    mod = _ada_mod(cvec, w_ada, b_ada)
    lat_row = lambda b: b
    ctx_row = lambda b: bsz

    tabs_a, tabs_b = _rope_tables(n, "A", False), _rope_tables(n, "B", False)
    tabs_id = _rope_tables(n_ctx, "A", True)
    na_tables = _na_bias_tables(c_rpb, rows)

    for l in range(depth):
        need_ctx = l < depth - 1
        lam_init = 0.8 - 0.6 * float(np.exp(-0.3 * l))
        m6 = [mod[l, :, k * d:(k + 1) * d].reshape(MOD_ROWS, 1, d) for k in range(N_MOD)]
        sh1, sc1, g1, sh2, sc2, g2 = m6
        w_in_l = w_in[l].astype(BF16)
        w_out_l = w_out[l].astype(BF16)
        wg, wu, wd = w_e_gate[l], w_e_up[l], w_e_down[l]

        hx = _norm_mod(x, norm1_g[l], sh1, sc1, lat_row, BF16).reshape(bsz * n, d)
        hc = _norm_mod(ctx, norm1_g[l], sh1, sc1, ctx_row, BF16).reshape(bsz * n_ctx, d)
        px = _in_proj(hx, w_in_l, tabs_a, tabs_b, b_q_norm_g[l], b_k_norm_g[l], n)
        pc = _in_proj(hc, w_in_l, tabs_id, tabs_id, b_q_norm_g[l], b_k_norm_g[l], n_ctx)

        common = dict(n_ctx=n_ctx, n_lat=n, unit_heads=unit_heads)
        a_args = dict(kv_heads=a_heads, group=1, q_unit=UNIT_AQ, k_unit=UNIT_AK, v_unit=UNIT_AV, diff=True,
                      lam=a_lambda[l], subln_g=a_subln_g[l], lam_init=lam_init)
        b_args = dict(kv_heads=unit_heads, group=b_heads // unit_heads, q_unit=UNIT_BQ, k_unit=UNIT_BK, v_unit=UNIT_BV)
        oa = _attention(px, pc, px, n_q=n, **a_args, **common)
        ob = _attention(px, pc, px, n_q=n, **b_args, **common)
        oc = _neighbourhood(px, pc, na_tables[l], n_lat=n, n_ctx=n_ctx, heads=c_heads, unit_heads=unit_heads)
        x = _out_proj(oa, ob, oc, w_out_l, x, g1, lat_row)
        x = _expert_ffn_residual(x, norm2_g[l], sh2, sc2, g2, lat_row, w_router[l], wg, wu, wd)

        if need_ctx:
            c_args = dict(kv_heads=c_heads, group=1, q_unit=UNIT_CQ, k_unit=UNIT_CK, v_unit=UNIT_CV)
            oa_c = _attention(pc, pc, None, n_q=n_ctx, **a_args, **common)
            ob_c = _attention(pc, pc, None, n_q=n_ctx, **b_args, **common)
            oc_c = _attention(pc, pc, None, n_q=n_ctx, **c_args, **common)
            ctx = _out_proj(oa_c, ob_c, oc_c, w_out_l, ctx, g1, ctx_row)
            ctx = _expert_ffn_residual(ctx, norm2_g[l], sh2, sc2, g2, ctx_row, w_router[l], wg, wu, wd)

    return _final_norm(x, final_g)
```

```python
import functools

import numpy as np
import jax
import jax.numpy as jnp
from jax import lax
from jax.experimental import pallas as pl
from jax.experimental.pallas import tpu as pltpu

HEAD_DIM = 128
A_QK_DIM = HEAD_DIM // 2
GRID_W = 64
NA_ROWS = 8
NA_COLS = 16
EC_CAPACITY = 2
N_MOD = 6
ROPE_BASE = 10000.0
EPS = 1e-6
NEG = -1e30
LOG2E = 1.4426950408889634
A_Q_SCALE = A_QK_DIM ** -0.5 * LOG2E
Q_SCALE = HEAD_DIM ** -0.5 * LOG2E
MIB = 1024 * 1024
MOD_ROWS = 8
F32 = jnp.float32
BF16 = jnp.bfloat16

UNIT_AQ, UNIT_AK, UNIT_AV = 0, 2, 4
UNIT_BQ, UNIT_BK, UNIT_BV = 6, 9, 10
UNIT_CQ, UNIT_CK, UNIT_CV = 11, 14, 17
N_UNITS = 20


def _cparams(sem, vmem_mib):
    return pltpu.CompilerParams(dimension_semantics=sem, vmem_limit_bytes=vmem_mib * MIB)


def _nt_dot(a, b):
    return lax.dot_general(a, b, (((1,), (1,)), ((), ())), preferred_element_type=F32)


def _ada_kernel(c_ref, w_ref, b_ref, o_ref):
    c = c_ref[...]
    s = (c * jax.nn.sigmoid(c)).astype(BF16)
    o_ref[...] = jnp.dot(s, w_ref[...].astype(BF16), preferred_element_type=F32) + b_ref[...]


def _ada_mod(cvec, w_ada, b_ada):
    depth, d, nm = w_ada.shape
    tn = min(512, nm)
    return pl.pallas_call(
        _ada_kernel,
        grid=(depth, nm // tn),
        in_specs=[pl.BlockSpec((MOD_ROWS, d), lambda l, j: (0, 0)),
                  pl.BlockSpec((None, d, tn), lambda l, j: (l, 0, j)),
                  pl.BlockSpec((None, 1, tn), lambda l, j: (l, 0, j))],
        out_specs=pl.BlockSpec((None, MOD_ROWS, tn), lambda l, j: (l, 0, j)),
        out_shape=jax.ShapeDtypeStruct((depth, MOD_ROWS, nm), F32),
        compiler_params=_cparams(("parallel", "parallel"), 40),
    )(cvec, w_ada, b_ada.reshape(depth, 1, nm))


def _rms(x, g):
    return x * lax.rsqrt(jnp.mean(x * x, axis=-1, keepdims=True) + EPS) * g


def _norm_mod_kernel(x_ref, g_ref, sh_ref, sc_ref, o_ref):
    y = _rms(x_ref[...], g_ref[...])
    o_ref[...] = (y * (1.0 + sc_ref[...]) + sh_ref[...]).astype(o_ref.dtype)


def _norm_kernel(x_ref, g_ref, o_ref):
    o_ref[...] = _rms(x_ref[...], g_ref[...]).astype(o_ref.dtype)


def _row_spec(d, row_fn):
    return pl.BlockSpec((None, 1, d), lambda b, i: (row_fn(b), 0, 0))


def _norm_mod(x, g, shift, scale, row_fn, out_dtype):
    bsz, n, d = x.shape
    tm = min(256, n)
    return pl.pallas_call(
        _norm_mod_kernel,
        grid=(bsz, n // tm),
        in_specs=[pl.BlockSpec((None, tm, d), lambda b, i: (b, i, 0)),
                  pl.BlockSpec((1, d), lambda b, i: (0, 0)),
                  _row_spec(d, row_fn), _row_spec(d, row_fn)],
        out_specs=pl.BlockSpec((None, tm, d), lambda b, i: (b, i, 0)),
        out_shape=jax.ShapeDtypeStruct((bsz, n, d), out_dtype),
        compiler_params=_cparams(("parallel", "parallel"), 32),
    )(x, g.reshape(1, d), shift, scale)


def _final_norm(x, g):
    bsz, n, d = x.shape
    tm = min(256, n)
    return pl.pallas_call(
        _norm_kernel,
        grid=(bsz, n // tm),
        in_specs=[pl.BlockSpec((None, tm, d), lambda b, i: (b, i, 0)),
                  pl.BlockSpec((1, d), lambda b, i: (0, 0))],
        out_specs=pl.BlockSpec((None, tm, d), lambda b, i: (b, i, 0)),
        out_shape=jax.ShapeDtypeStruct((bsz, n, d), x.dtype),
        compiler_params=_cparams(("parallel", "parallel"), 32),
    )(x, g.reshape(1, d))


def _rope_tables(n, group, identity):
    if identity:
        return jnp.ones((n, HEAD_DIM), F32), jnp.zeros((n, HEAD_DIM), F32)
    p = jnp.arange(HEAD_DIM)
    half = 16 if group == "A" else 32
    use_col = ((p // (2 * half)) % 2) == 1
    first = (p % (2 * half)) < half
    freqs = ROPE_BASE ** (-(p % half).astype(F32) / half)
    t = jnp.arange(n)
    pos = jnp.where(use_col[None, :], (t % GRID_W)[:, None], (t // GRID_W)[:, None]).astype(F32)
    ang = pos * freqs[None, :]
    return jnp.cos(ang), jnp.where(first[None, :], -jnp.sin(ang), jnp.sin(ang))


def _rotate(x, cos, sin_signed, half):
    lane = lax.broadcasted_iota(jnp.int32, x.shape, 1)
    first = (lane % (2 * half)) < half
    partner = jnp.where(first, pltpu.roll(x, HEAD_DIM - half, 1), pltpu.roll(x, half, 1))
    return x * cos + partner * sin_signed


def _inproj_kernel(h_ref, w_ref, ca_ref, sa_ref, cb_ref, sb_ref, gq_ref, gk_ref, o_ref, wb_ref):
    j = pl.program_id(0)

    @pl.when(pl.program_id(1) == 0)
    def _():
        wb_ref[...] = w_ref[...].astype(BF16)

    acc = jnp.dot(h_ref[...], wb_ref[...], preferred_element_type=F32)
    nh = acc.shape[1] // HEAD_DIM
    heads = [slice(i * HEAD_DIM, (i + 1) * HEAD_DIM) for i in range(nh)]

    @pl.when(j < UNIT_AV)
    def _():
        scale = jnp.where(j < UNIT_AK, A_Q_SCALE, 1.0).astype(F32)
        for s in heads:
            o_ref[:, s] = (_rotate(acc[:, s], ca_ref[...], sa_ref[...], 16) * scale).astype(o_ref.dtype)

    @pl.when((j >= UNIT_BQ) & (j < UNIT_BV))
    def _():
        is_q = j < UNIT_BK
        g = jnp.where(is_q, gq_ref[...], gk_ref[...])
        scale = jnp.where(is_q, Q_SCALE, 1.0).astype(F32)
        for s in heads:
            y = _rms(acc[:, s], g)
            o_ref[:, s] = (_rotate(y, cb_ref[...], sb_ref[...], 32) * scale).astype(o_ref.dtype)

    @pl.when((j >= UNIT_CQ) & (j < UNIT_CK))
    def _():
        o_ref[...] = (acc * Q_SCALE).astype(o_ref.dtype)

    @pl.when(((j >= UNIT_AV) & (j < UNIT_BQ)) | (j == UNIT_BV) | (j >= UNIT_CK))
    def _():
        o_ref[...] = acc.astype(o_ref.dtype)


def _in_proj(h, w, tabs_a, tabs_b, gq, gk, n_per_sample):
    w_all, layer = w
    m, d = h.shape
    wid = w_all.shape[2]
    tn = wid // N_UNITS
    tm = min(1024, n_per_sample)
    tiles = n_per_sample // tm
    tab = pl.BlockSpec((tm, HEAD_DIM), lambda j, i: (i % tiles, 0))
    vec = pl.BlockSpec((1, HEAD_DIM), lambda j, i: (0, 0))
    return pl.pallas_call(
        _inproj_kernel,
        grid=(N_UNITS, m // tm),
        in_specs=[pl.BlockSpec((tm, d), lambda j, i: (i, 0)),
                  pl.BlockSpec((None, d, tn), lambda j, i: (layer, 0, j)),
                  tab, tab, tab, tab, vec, vec],
        out_specs=pl.BlockSpec((tm, tn), lambda j, i: (i, j)),
        out_shape=jax.ShapeDtypeStruct((m, wid), BF16),
        scratch_shapes=[pltpu.VMEM((d, tn), BF16)],
        compiler_params=_cparams(("arbitrary", "arbitrary"), 56),
    )(h, w_all, tabs_a[0], tabs_a[1], tabs_b[0], tabs_b[1], gq.reshape(1, HEAD_DIM), gk.reshape(1, HEAD_DIM))


def _probs(q, kc, kl):
    s_c = _nt_dot(q, kc)
    m = jnp.max(s_c, axis=-1, keepdims=True)
    if kl is None:
        p_c = jnp.exp2(s_c - m)
        return p_c, None, jnp.sum(p_c, axis=-1, keepdims=True)
    s_l = _nt_dot(q, kl)
    m = jnp.maximum(m, jnp.max(s_l, axis=-1, keepdims=True))
    p_c = jnp.exp2(s_c - m)
    p_l = jnp.exp2(s_l - m)
    return p_c, p_l, jnp.sum(p_c, axis=-1, keepdims=True) + jnp.sum(p_l, axis=-1, keepdims=True)


def _pv(p_c, p_l, vc, vl):
    o = jnp.dot(p_c.astype(BF16), vc, preferred_element_type=F32)
    if p_l is not None:
        o = o + jnp.dot(p_l.astype(BF16), vl, preferred_element_type=F32)
    return o


def _attn_kernel(*refs, diff, has_lat, group, lam_init):
    it = iter(refs)
    q_ref, kc_ref, vc_ref = next(it), next(it), next(it)
    kl = vl = lam_ref = g_ref = None
    if has_lat:
        kl, vl = next(it)[...], next(it)[...]
    if diff:
        lam_ref, g_ref = next(it), next(it)
    o_ref = next(it)
    kc, vc = kc_ref[...], vc_ref[...]

    for g in range(group):
        cols = slice(g * HEAD_DIM, (g + 1) * HEAD_DIM)
        q = q_ref[:, cols]
        if diff:
            qf = q.astype(F32)
            lane = lax.broadcasted_iota(jnp.int32, qf.shape, 1)
            p1c, p1l, l1 = _probs(jnp.where(lane < A_QK_DIM, qf, 0.0).astype(BF16), kc, kl)
            p2c, p2l, l2 = _probs(jnp.where(lane >= A_QK_DIM, qf, 0.0).astype(BF16), kc, kl)
            lv = lam_ref[...]
            lam = (jnp.exp(jnp.sum(lv[0:1] * lv[1:2], axis=-1, keepdims=True))
                   - jnp.exp(jnp.sum(lv[2:3] * lv[3:4], axis=-1, keepdims=True)) + lam_init)
            a1, a2 = 1.0 / l1, lam * (1.0 / l2)
            o = _pv(p1c * a1 - p2c * a2, None if p1l is None else p1l * a1 - p2l * a2, vc, vl)
            o = _rms(o, g_ref[...]) * (1.0 - lam_init)
        else:
            p_c, p_l, l1 = _probs(q, kc, kl)
            o = _pv(p_c, p_l, vc, vl) * (1.0 / l1)
        o_ref[:, cols] = o.astype(o_ref.dtype)


def _attention(pq, pc, px, *, n_q, n_ctx, n_lat, kv_heads, group, q_unit, k_unit, v_unit, unit_heads,
               diff=False, lam=None, subln_g=None, lam_init=0.0):
    bsz = pq.shape[0] // n_q
    has_lat = px is not None
    tq = min(256, n_q)
    nqt = n_q // tq
    gw = group * HEAD_DIM
    qc, kc, vc = q_unit * unit_heads // group, k_unit * unit_heads, v_unit * unit_heads
    assert (q_unit * unit_heads) % group == 0
    in_specs = [pl.BlockSpec((tq, gw), lambda b, h, i: (b * nqt + i, qc + h)),
                pl.BlockSpec((n_ctx, HEAD_DIM), lambda b, h, i: (b, kc + h)),
                pl.BlockSpec((n_ctx, HEAD_DIM), lambda b, h, i: (b, vc + h))]
    args = [pq, pc, pc]
    if has_lat:
        in_specs += [pl.BlockSpec((n_lat, HEAD_DIM), lambda b, h, i: (b, kc + h)),
                     pl.BlockSpec((n_lat, HEAD_DIM), lambda b, h, i: (b, vc + h))]
        args += [px, px]
    if diff:
        in_specs += [pl.BlockSpec((4, A_QK_DIM), lambda b, h, i: (0, 0)),
                     pl.BlockSpec((1, HEAD_DIM), lambda b, h, i: (0, 0))]
        args += [lam, subln_g.reshape(1, HEAD_DIM)]
    return pl.pallas_call(
        functools.partial(_attn_kernel, diff=diff, has_lat=has_lat, group=group, lam_init=lam_init),
        grid=(bsz, kv_heads, nqt),
        in_specs=in_specs,
        out_specs=pl.BlockSpec((tq, gw), lambda b, h, i: (b * nqt + i, h)),
        out_shape=jax.ShapeDtypeStruct((bsz * n_q, kv_heads * gw), BF16),
        compiler_params=_cparams(("parallel", "parallel", "arbitrary"), 48),
    )(*args)


NA_QROWS = 4


def _na_plan(rows):
    kh = min(NA_ROWS, rows)
    qb = min(NA_QROWS, rows)
    assert rows % qb == 0
    kw = min(rows, kh + qb)
    starts, var_ids, variants = [], [], []
    for r0 in range(0, rows, qb):
        rs = [int(np.clip(r - kh // 2, 0, rows - kh)) for r in range(r0, r0 + qb)]
        kstart = min(min(rs), rows - kw)
        assert kstart >= 0 and max(rs) + kh <= kstart + kw
        sig = tuple((r0 + u - kstart, rs[u] - kstart) for u in range(qb))
        if sig not in variants:
            variants.append(sig)
        starts.append(kstart)
        var_ids.append(variants.index(sig))
    return kw, starts, var_ids, variants


def _na_toeplitz(c_rpb):
    q = np.arange(GRID_W)
    kc = np.arange(GRID_W)
    start = np.clip(q - NA_COLS // 2, 0, GRID_W - NA_COLS)
    inwin = (kc[None, :] >= start[:, None]) & (kc[None, :] < start[:, None] + NA_COLS)
    ci = kc[None, :] - q[:, None] + NA_COLS - 1
    onehot = ((ci[..., None] == np.arange(2 * NA_COLS - 1)) & inwin[..., None]).astype(np.float32)
    toep = jnp.einsum("lhab,qcb->lhaqc", c_rpb.astype(F32), jnp.asarray(onehot), precision=lax.Precision.HIGHEST)
    return jnp.where(jnp.asarray(inwin)[None, None, None], toep * LOG2E, NEG)


def _na_kernel(kst_ref, var_ref, q_ref, k_ref, v_ref, kc_ref, vc_ref, toep_ref, o_ref, t_ref, *,
               nblk, qlen, klen, variants, kh):
    kctx = kc_ref[...]
    vctx = vc_ref[...]

    masked = jnp.full((GRID_W, GRID_W), NEG, F32)
    for vi, sig in enumerate(variants):
        for u, (qoff, woff) in enumerate(sig):
            tile = lambda kr: toep_ref[kr - qoff + NA_ROWS - 1] if woff <= kr < woff + kh else masked
            for kr in range(0, klen // GRID_W, 2):
                t_ref[vi, u * GRID_W:(u + 1) * GRID_W, kr * GRID_W:(kr + 2) * GRID_W] = jnp.concatenate(
                    [tile(kr), tile(kr + 1)], axis=1)

    def body(rb, carry):
        qoff = pl.multiple_of(rb * qlen, qlen)
        koff = pl.multiple_of(kst_ref[rb] * GRID_W, GRID_W)
        q = q_ref[pl.ds(qoff, qlen), :]
        s_loc = _nt_dot(q, k_ref[pl.ds(koff, klen), :]) + t_ref[var_ref[rb]]
        s_ctx = _nt_dot(q, kctx)
        m = jnp.maximum(jnp.max(s_loc, axis=-1, keepdims=True), jnp.max(s_ctx, axis=-1, keepdims=True))
        p_loc = jnp.exp2(s_loc - m)
        p_ctx = jnp.exp2(s_ctx - m)
        l = jnp.sum(p_loc, axis=-1, keepdims=True) + jnp.sum(p_ctx, axis=-1, keepdims=True)
        o = (jnp.dot(p_loc.astype(BF16), v_ref[pl.ds(koff, klen), :], preferred_element_type=F32)
             + jnp.dot(p_ctx.astype(BF16), vctx, preferred_element_type=F32))
        o_ref[pl.ds(qoff, qlen), :] = (o * (1.0 / l)).astype(o_ref.dtype)
        return carry

    lax.fori_loop(0, nblk, body, 0, unroll=2 if nblk % 2 == 0 else 1)


def _neighbourhood(px, pc, toep, *, n_lat, n_ctx, heads, unit_heads):
    bsz = px.shape[0] // n_lat
    rows = n_lat // GRID_W
    kw, starts, var_ids, variants = _na_plan(rows)
    assert kw % 2 == 0
    nblk = len(starts)
    qlen, klen = n_lat // nblk, kw * GRID_W
    qc, kc, vc = UNIT_CQ * unit_heads, UNIT_CK * unit_heads, UNIT_CV * unit_heads
    lat = lambda c0: pl.BlockSpec((n_lat, HEAD_DIM), lambda b, h, ks, vi: (b, c0 + h))
    ctx = lambda c0: pl.BlockSpec((n_ctx, HEAD_DIM), lambda b, h, ks, vi: (b, c0 + h))
    return pl.pallas_call(
        functools.partial(_na_kernel, nblk=nblk, qlen=qlen, klen=klen, variants=tuple(variants),
                          kh=min(NA_ROWS, rows)),
        grid_spec=pltpu.PrefetchScalarGridSpec(
            num_scalar_prefetch=2,
            grid=(bsz, heads),
            in_specs=[lat(qc), lat(kc), lat(vc), ctx(kc), ctx(vc),
                      pl.BlockSpec((None,) + toep.shape[1:], lambda b, h, ks, vi: (h, 0, 0, 0))],
            out_specs=pl.BlockSpec((n_lat, HEAD_DIM), lambda b, h, ks, vi: (b, h)),
            scratch_shapes=[pltpu.VMEM((len(variants), qlen, klen), F32)]),
        out_shape=jax.ShapeDtypeStruct((bsz * n_lat, heads * HEAD_DIM), BF16),
        compiler_params=_cparams(("parallel", "parallel"), 40),
    )(jnp.asarray(starts, jnp.int32), jnp.asarray(var_ids, jnp.int32), px, px, px, pc, pc, toep)


def _outproj_kernel(oa_ref, ob_ref, oc_ref, w_ref, x_ref, g_ref, o_ref, wb_ref):
    @pl.when((pl.program_id(1) == 0) & (pl.program_id(2) == 0))
    def _():
        wb_ref[...] = w_ref[...].astype(BF16)

    wa, wb = oa_ref.shape[1], ob_ref.shape[1]
    acc = jnp.dot(oa_ref[...], wb_ref[0:wa, :], preferred_element_type=F32)
    acc += jnp.dot(ob_ref[...], wb_ref[wa:wa + wb, :], preferred_element_type=F32)
    acc += jnp.dot(oc_ref[...], wb_ref[wa + wb:, :], preferred_element_type=F32)
    o_ref[...] = x_ref[...] + g_ref[...] * acc


def _out_proj(oa, ob, oc, w, x, gate, row_fn):
    w_all, layer = w
    kdim = w_all.shape[1]
    bsz, n, d = x.shape
    tm = min(1024, n)
    tn = min(512, d)
    nt = n // tm
    lhs = lambda a: pl.BlockSpec((tm, a.shape[1]), lambda j, b, i: (b * nt + i, 0))
    return pl.pallas_call(
        _outproj_kernel,
        grid=(d // tn, bsz, nt),
        in_specs=[lhs(oa), lhs(ob), lhs(oc),
                  pl.BlockSpec((None, kdim, tn), lambda j, b, i: (layer, 0, j)),
                  pl.BlockSpec((None, tm, tn), lambda j, b, i: (b, i, j)),
                  pl.BlockSpec((None, 1, tn), lambda j, b, i: (row_fn(b), 0, j))],
        out_specs=pl.BlockSpec((None, tm, tn), lambda j, b, i: (b, i, j)),
        out_shape=jax.ShapeDtypeStruct((bsz, n, d), F32),
        scratch_shapes=[pltpu.VMEM((kdim, tn), BF16)],
        compiler_params=_cparams(("arbitrary", "arbitrary", "arbitrary"), 56),
    )(oa, ob, oc, w_all, x, gate)


def _split3(x):
    hi = x.astype(BF16)
    r = x - hi.astype(F32)
    mid = r.astype(BF16)
    lo = (r - mid.astype(F32)).astype(BF16)
    return hi, mid, lo


def _router_kernel(x_ref, g_ref, sh_ref, sc_ref, wh_ref, wl_ref, h_ref, afft_ref, aff_ref):
    y = _rms(x_ref[...], g_ref[...])
    h = y * (1.0 + sc_ref[...]) + sh_ref[...]
    h_ref[...] = h
    hh = h.astype(BF16)
    hl = (h - hh.astype(F32)).astype(BF16)
    wh, wl = wh_ref[...], wl_ref[...]
    lt = _nt_dot(wh, hh) + _nt_dot(wh, hl) + _nt_dot(wl, hh)
    e = jnp.exp(lt - jnp.max(lt, axis=0, keepdims=True))
    afft_ref[...] = e / jnp.sum(e, axis=0, keepdims=True)
    ln = _nt_dot(hh, wh) + _nt_dot(hl, wh) + _nt_dot(hh, wl)
    en = jnp.exp(ln - jnp.max(ln, axis=-1, keepdims=True))
    aff_ref[...] = en / jnp.sum(en, axis=-1, keepdims=True)


def _router(x, g, shift, scale, row_fn, w_router):
    bsz, n, d = x.shape
    ne = w_router.shape[1]
    tm = min(256, n)
    wt = w_router.T
    wh = wt.astype(BF16)
    wl = (wt - wh.astype(F32)).astype(BF16)
    wspec = pl.BlockSpec((ne, d), lambda b, i: (0, 0))
    return pl.pallas_call(
        _router_kernel,
        grid=(bsz, n // tm),
        in_specs=[pl.BlockSpec((None, tm, d), lambda b, i: (b, i, 0)),
                  pl.BlockSpec((1, d), lambda b, i: (0, 0)),
                  _row_spec(d, row_fn), _row_spec(d, row_fn), wspec, wspec],
        out_specs=[pl.BlockSpec((None, tm, d), lambda b, i: (b, i, 0)),
                   pl.BlockSpec((None, ne, tm), lambda b, i: (b, 0, i)),
                   pl.BlockSpec((None, tm, ne), lambda b, i: (b, i, 0))],
        out_shape=[jax.ShapeDtypeStruct((bsz, n, d), F32),
                   jax.ShapeDtypeStruct((bsz, ne, n), F32),
                   jax.ShapeDtypeStruct((bsz, n, ne), F32)],
        compiler_params=_cparams(("parallel", "parallel"), 40),
    )(x, g.reshape(1, d), shift, scale, wh, wl)


def _lane_cumsum(x01, tri):
    r, n = x01.shape
    carry = jnp.zeros((r, 1), F32)
    out = []
    for c in range(n // HEAD_DIM):
        blk = x01[:, c * HEAD_DIM:(c + 1) * HEAD_DIM].astype(BF16)
        cs = jnp.dot(blk, tri, preferred_element_type=F32) + carry
        out.append(cs)
        carry = cs[:, HEAD_DIM - 1:HEAD_DIM]
    return jnp.concatenate(out, axis=1) if len(out) > 1 else out[0]


def _select_kernel(afft_ref, r_ref, o_ref, *, cap):
    a = afft_ref[...]
    ne, n = a.shape
    bits = pltpu.bitcast(a, jnp.int32)

    def bisect(_, lohi):
        lo, hi = lohi
        mid = lo + lax.shift_right_logical(hi - lo, 1)
        cnt = jnp.sum((bits >= mid).astype(F32), axis=1, keepdims=True)
        ok = cnt >= cap
        return jnp.where(ok, mid, lo), jnp.where(ok, hi, mid)

    lo0 = jnp.zeros((ne, 1), jnp.int32)
    hi0 = jnp.full((ne, 1), 0x7F800000, jnp.int32)
    thr, _ = lax.fori_loop(0, 32, bisect, (lo0, hi0))

    ri = lax.broadcasted_iota(jnp.int32, (HEAD_DIM, HEAD_DIM), 0)
    ci = lax.broadcasted_iota(jnp.int32, (HEAD_DIM, HEAD_DIM), 1)
    tri = (ri <= ci).astype(BF16)
    gt = (bits > thr).astype(F32)
    eq = (bits == thr).astype(F32)
    need = cap - jnp.sum(gt, axis=1, keepdims=True)
    sel = gt + eq * (_lane_cumsum(eq, tri) <= need).astype(F32)
    pos = _lane_cumsum(sel, tri) - 1.0

    slot = lax.broadcasted_iota(jnp.int32, (cap, n), 0).astype(F32)
    rmat = r_ref[...]
    for e in range(ne):
        onehot = jnp.where((pos[e:e + 1, :] == slot) & (sel[e:e + 1, :] > 0.5), 1.0, 0.0).astype(BF16)
        o_ref[e] = jnp.dot(onehot, rmat, preferred_element_type=F32)


def _select(afft, aff, cap):
    bsz, ne, n = afft.shape
    t = jnp.arange(n)
    hi, mid, lo = _split3(aff)
    tcols = jnp.stack([t // 64, t % 64], axis=1).astype(BF16)
    pad = jnp.zeros((bsz, n, HEAD_DIM - 2 - 3 * ne), BF16)
    rmat = jnp.concatenate([jnp.broadcast_to(tcols[None], (bsz, n, 2)), hi, mid, lo, pad], axis=2)
    tab = pl.pallas_call(
        functools.partial(_select_kernel, cap=cap),
        grid=(bsz,),
        in_specs=[pl.BlockSpec((None, ne, n), lambda b: (b, 0, 0)),
                  pl.BlockSpec((None, n, HEAD_DIM), lambda b: (b, 0, 0))],
        out_specs=pl.BlockSpec((None, ne, cap, HEAD_DIM), lambda b: (b, 0, 0, 0)),
        out_shape=jax.ShapeDtypeStruct((bsz, ne, cap, HEAD_DIM), F32),
        compiler_params=_cparams(("parallel",), 40),
    )(afft, rmat)
    idx = (tab[..., 0] * 64.0 + tab[..., 1]).astype(jnp.int32)
    parts = tab[..., 2:2 + 3 * ne].reshape(bsz, ne, cap, 3, ne).sum(axis=3)
    gates = (parts * jnp.eye(ne, dtype=F32)[None, :, None, :]).sum(axis=-1, keepdims=True)
    return idx, gates


def _ffn_up_kernel(idx_ref, h_hbm, wg_ref, wu_ref, o_ref, xg_ref, sem, *, cap, ne, nb):
    e, b = pl.program_id(0), pl.program_id(1)
    step = e * nb + b
    slot = lax.rem(step, 2)

    def gather(e_, b_, slot_):
        base = (b_ * ne + e_) * cap
        for j in range(cap):
            pltpu.make_async_copy(h_hbm.at[b_, pl.ds(idx_ref[base + j], 1), :],
                                  xg_ref.at[slot_, pl.ds(j, 1), :], sem.at[slot_]).start()

    @pl.when(step == 0)
    def _():
        gather(e, b, 0)

    pltpu.make_async_copy(h_hbm.at[b, pl.ds(0, cap), :], xg_ref.at[slot], sem.at[slot]).wait()

    last = step + 1 >= ne * nb
    nxt = jnp.where(last, step, step + 1)
    gather(lax.div(nxt, nb), lax.rem(nxt, nb), 1 - slot)

    x = xg_ref[slot].astype(BF16)
    gate = jnp.dot(x, wg_ref[...].astype(BF16), preferred_element_type=F32)
    up = jnp.dot(x, wu_ref[...].astype(BF16), preferred_element_type=F32)
    o_ref[...] = (gate * jax.nn.sigmoid(gate) * up).astype(o_ref.dtype)

    @pl.when(last)
    def _():
        pltpu.make_async_copy(h_hbm.at[b, pl.ds(0, cap), :], xg_ref.at[1 - slot], sem.at[1 - slot]).wait()


def _ffn_up(idx_flat, h, wg, wu, cap):
    bsz, n, d = h.shape
    (wg_all, layer), (wu_all, _) = wg, wu
    ne, _, ff = wg_all.shape[1:]
    wspec = pl.BlockSpec((None, None, d, ff), lambda e, b, idx: (layer, e, 0, 0))
    return pl.pallas_call(
        functools.partial(_ffn_up_kernel, cap=cap, ne=ne, nb=bsz),
        grid_spec=pltpu.PrefetchScalarGridSpec(
            num_scalar_prefetch=1,
            grid=(ne, bsz),
            in_specs=[pl.BlockSpec(memory_space=pl.ANY), wspec, wspec],
            out_specs=pl.BlockSpec((None, None, cap, ff), lambda e, b, idx: (b, e, 0, 0)),
            scratch_shapes=[pltpu.VMEM((2, cap, d), F32), pltpu.SemaphoreType.DMA((2,))]),
        out_shape=jax.ShapeDtypeStruct((bsz, ne, cap, ff), BF16),
        compiler_params=_cparams(("arbitrary", "arbitrary"), 58),
    )(idx_flat, h, wg_all, wu_all)


ACC_FULL_WIDTH_BYTES = 8 * MIB
SCATTER_ROWS = 4


def _ffn_down_kernel(idx_ref, hid_ref, wd_ref, gate_ref, x_ref, g2_ref, o_ref, acc_ref, ye_ref, *, cap, ne, tr):
    b, e = pl.program_id(0), pl.program_id(2)

    @pl.when(e == 0)
    def _():
        acc_ref[...] = jnp.zeros_like(acc_ref)

    @pl.when(e < ne)
    def _():
        ye_ref[...] = jnp.dot(hid_ref[...], wd_ref[...].astype(BF16), preferred_element_type=F32) * gate_ref[...]
        base = (b * ne + e) * cap

        def group(gi, c):
            j0 = gi * SCATTER_ROWS
            toks = [idx_ref[base + j0 + u] for u in range(SCATTER_ROWS)]
            rows = [acc_ref[pl.ds(t, 1), :] for t in toks]
            for u, (t, r) in enumerate(zip(toks, rows)):
                acc_ref[pl.ds(t, 1), :] = r + ye_ref[pl.ds(j0 + u, 1), :]
            return c

        lax.fori_loop(0, cap // SCATTER_ROWS, group, 0, unroll=2)

    @pl.when(e >= ne)
    def _():
        r0 = pl.multiple_of((e - ne) * tr, tr)
        o_ref[...] = x_ref[...] + g2_ref[...] * acc_ref[pl.ds(r0, tr), :]


def _ffn_down(idx_flat, hid, wd, gates, x, g2, row_fn):
    bsz, n, d = x.shape
    wd_all, layer = wd
    ne, ff = wd_all.shape[1:3]
    cap = hid.shape[2]
    assert cap % SCATTER_ROWS == 0
    cb = d if n * d * 4 <= ACC_FULL_WIDTH_BYTES else min(1024, d)
    tr = min(512, n)
    ex = lambda e: jnp.minimum(e, ne - 1)
    rt = lambda e: jnp.maximum(e - ne, 0)
    return pl.pallas_call(
        functools.partial(_ffn_down_kernel, cap=cap, ne=ne, tr=tr),
        grid_spec=pltpu.PrefetchScalarGridSpec(
            num_scalar_prefetch=1,
            grid=(bsz, d // cb, ne + n // tr),
            in_specs=[pl.BlockSpec((None, None, cap, ff), lambda b, c, e, idx: (b, ex(e), 0, 0)),
                      pl.BlockSpec((None, None, ff, cb), lambda b, c, e, idx: (layer, ex(e), 0, c)),
                      pl.BlockSpec((None, None, cap, 1), lambda b, c, e, idx: (b, ex(e), 0, 0)),
                      pl.BlockSpec((None, tr, cb), lambda b, c, e, idx: (b, rt(e), c)),
                      pl.BlockSpec((None, 1, cb), lambda b, c, e, idx: (row_fn(b), 0, c))],
            out_specs=pl.BlockSpec((None, tr, cb), lambda b, c, e, idx: (b, rt(e), c)),
            scratch_shapes=[pltpu.VMEM((n, cb), F32), pltpu.VMEM((cap, cb), F32)]),
        out_shape=jax.ShapeDtypeStruct((bsz, n, d), F32),
        compiler_params=_cparams(("parallel", "parallel", "arbitrary"), 48),
    )(idx_flat, hid, wd_all, gates, x, g2)


def _expert_ffn_residual(x, norm_g, shift, scale, g2, row_fn, w_router, wg, wu, wd):
    n = x.shape[1]
    ne = w_router.shape[1]
    cap = EC_CAPACITY * n // ne
    h, afft, aff = _router(x, norm_g, shift, scale, row_fn, w_router)
    idx, gates = _select(afft, aff, cap)
    idx_flat = idx.reshape(-1)
    hid = _ffn_up(idx_flat, h, wg, wu, cap)
    return _ffn_down(idx_flat, hid, wd, gates, x, g2, row_fn)


def kernel(x, c, ctx, c_ctx, w_ada, b_ada, norm1_g, norm2_g, w_in, w_out, a_lambda, a_subln_g, b_q_norm_g,
           b_k_norm_g, c_rpb, w_router, w_e_gate, w_e_up, w_e_down, final_g):
    bsz, n, d = x.shape
    n_ctx = ctx.shape[1]
    depth = w_ada.shape[0]
    rows = n // GRID_W
    unit_heads = d // 8 // HEAD_DIM
    a_heads, b_heads, c_heads = 2 * unit_heads, 3 * unit_heads, 3 * unit_heads
    assert bsz + 1 <= MOD_ROWS and rows >= 1 and n % GRID_W == 0

    cvec = jnp.zeros((MOD_ROWS, d), F32).at[:bsz].set(c).at[bsz].set(c_ctx)
    mod = _ada_mod(cvec, w_ada, b_ada)
    lat_row = lambda b: b
    ctx_row = lambda b: bsz

    tabs_a, tabs_b = _rope_tables(n, "A", False), _rope_tables(n, "B", False)
    tabs_id = _rope_tables(n_ctx, "A", True)
    na_toep = _na_toeplitz(c_rpb)

    for l in range(depth):
        need_ctx = l < depth - 1
        lam_init = 0.8 - 0.6 * float(np.exp(-0.3 * l))
        m6 = [mod[l, :, k * d:(k + 1) * d].reshape(MOD_ROWS, 1, d) for k in range(N_MOD)]
        sh1, sc1, g1, sh2, sc2, g2 = m6
        w_in_l, w_out_l = (w_in, l), (w_out, l)
        wg, wu, wd = (w_e_gate, l), (w_e_up, l), (w_e_down, l)

        hx = _norm_mod(x, norm1_g[l], sh1, sc1, lat_row, BF16).reshape(bsz * n, d)
        hc = _norm_mod(ctx, norm1_g[l], sh1, sc1, ctx_row, BF16).reshape(bsz * n_ctx, d)
        px = _in_proj(hx, w_in_l, tabs_a, tabs_b, b_q_norm_g[l], b_k_norm_g[l], n)
        pc = _in_proj(hc, w_in_l, tabs_id, tabs_id, b_q_norm_g[l], b_k_norm_g[l], n_ctx)

        common = dict(n_ctx=n_ctx, n_lat=n, unit_heads=unit_heads)
        a_args = dict(kv_heads=a_heads, group=1, q_unit=UNIT_AQ, k_unit=UNIT_AK, v_unit=UNIT_AV, diff=True,
                      lam=a_lambda[l], subln_g=a_subln_g[l], lam_init=lam_init)
        b_args = dict(kv_heads=unit_heads, group=b_heads // unit_heads, q_unit=UNIT_BQ, k_unit=UNIT_BK, v_unit=UNIT_BV)
        oa = _attention(px, pc, px, n_q=n, **a_args, **common)
        ob = _attention(px, pc, px, n_q=n, **b_args, **common)
        oc = _neighbourhood(px, pc, na_toep[l], n_lat=n, n_ctx=n_ctx, heads=c_heads, unit_heads=unit_heads)
        x = _out_proj(oa, ob, oc, w_out_l, x, g1, lat_row)
        x = _expert_ffn_residual(x, norm2_g[l], sh2, sc2, g2, lat_row, w_router[l], wg, wu, wd)

        if need_ctx:
            c_args = dict(kv_heads=c_heads, group=1, q_unit=UNIT_CQ, k_unit=UNIT_CK, v_unit=UNIT_CV)
            oa_c = _attention(pc, pc, None, n_q=n_ctx, **a_args, **common)
            ob_c = _attention(pc, pc, None, n_q=n_ctx, **b_args, **common)
            oc_c = _attention(pc, pc, None, n_q=n_ctx, **c_args, **common)
            ctx = _out_proj(oa_c, ob_c, oc_c, w_out_l, ctx, g1, ctx_row)
            ctx = _expert_ffn_residual(ctx, norm2_g[l], sh2, sc2, g2, ctx_row, w_router[l], wg, wu, wd)

    return _final_norm(x, final_g)
```

```python
import functools

import numpy as np
import jax
import jax.numpy as jnp
from jax import lax
from jax.experimental import pallas as pl
from jax.experimental.pallas import tpu as pltpu

HEAD_DIM = 128
A_QK_DIM = HEAD_DIM // 2
GRID_W = 64
NA_ROWS = 8
NA_COLS = 16
EC_CAPACITY = 2
N_MOD = 6
ROPE_BASE = 10000.0
EPS = 1e-6
NEG = -1e30
LOG2E = 1.4426950408889634
A_Q_SCALE = A_QK_DIM ** -0.5 * LOG2E
Q_SCALE = HEAD_DIM ** -0.5 * LOG2E
MIB = 1024 * 1024
MOD_ROWS = 8
F32 = jnp.float32
BF16 = jnp.bfloat16

UNIT_AQ, UNIT_AK, UNIT_AV = 0, 2, 4
UNIT_BQ, UNIT_BK, UNIT_BV = 6, 9, 10
UNIT_CQ, UNIT_CK, UNIT_CV = 11, 14, 17
N_UNITS = 20


def _cparams(sem, vmem_mib):
    return pltpu.CompilerParams(dimension_semantics=sem, vmem_limit_bytes=vmem_mib * MIB)


def _nt_dot(a, b):
    return lax.dot_general(a, b, (((1,), (1,)), ((), ())), preferred_element_type=F32)


def _ada_kernel(c_ref, w_ref, b_ref, o_ref):
    c = c_ref[...]
    s = (c * jax.nn.sigmoid(c)).astype(BF16)
    o_ref[...] = jnp.dot(s, w_ref[...].astype(BF16), preferred_element_type=F32) + b_ref[...]


def _ada_mod(cvec, w_ada, b_ada):
    depth, d, nm = w_ada.shape
    tn = min(512, nm)
    return pl.pallas_call(
        _ada_kernel,
        grid=(depth, nm // tn),
        in_specs=[pl.BlockSpec((MOD_ROWS, d), lambda l, j: (0, 0)),
                  pl.BlockSpec((None, d, tn), lambda l, j: (l, 0, j)),
                  pl.BlockSpec((None, 1, tn), lambda l, j: (l, 0, j))],
        out_specs=pl.BlockSpec((None, MOD_ROWS, tn), lambda l, j: (l, 0, j)),
        out_shape=jax.ShapeDtypeStruct((depth, MOD_ROWS, nm), F32),
        compiler_params=_cparams(("parallel", "parallel"), 40),
    )(cvec, w_ada, b_ada.reshape(depth, 1, nm))


def _rms(x, g):
    return x * lax.rsqrt(jnp.mean(x * x, axis=-1, keepdims=True) + EPS) * g


def _norm_mod_kernel(x_ref, g_ref, sh_ref, sc_ref, o_ref):
    y = _rms(x_ref[...], g_ref[...])
    o_ref[...] = (y * (1.0 + sc_ref[...]) + sh_ref[...]).astype(o_ref.dtype)


def _norm_kernel(x_ref, g_ref, o_ref):
    o_ref[...] = _rms(x_ref[...], g_ref[...]).astype(o_ref.dtype)


def _row_spec(d, row_fn):
    return pl.BlockSpec((None, 1, d), lambda b, i: (row_fn(b), 0, 0))


def _norm_mod(x, g, shift, scale, row_fn, out_dtype):
    bsz, n, d = x.shape
    tm = min(256, n)
    return pl.pallas_call(
        _norm_mod_kernel,
        grid=(bsz, n // tm),
        in_specs=[pl.BlockSpec((None, tm, d), lambda b, i: (b, i, 0)),
                  pl.BlockSpec((1, d), lambda b, i: (0, 0)),
                  _row_spec(d, row_fn), _row_spec(d, row_fn)],
        out_specs=pl.BlockSpec((None, tm, d), lambda b, i: (b, i, 0)),
        out_shape=jax.ShapeDtypeStruct((bsz, n, d), out_dtype),
        compiler_params=_cparams(("parallel", "parallel"), 32),
    )(x, g.reshape(1, d), shift, scale)


def _final_norm(x, g):
    bsz, n, d = x.shape
    tm = min(256, n)
    return pl.pallas_call(
        _norm_kernel,
        grid=(bsz, n // tm),
        in_specs=[pl.BlockSpec((None, tm, d), lambda b, i: (b, i, 0)),
                  pl.BlockSpec((1, d), lambda b, i: (0, 0))],
        out_specs=pl.BlockSpec((None, tm, d), lambda b, i: (b, i, 0)),
        out_shape=jax.ShapeDtypeStruct((bsz, n, d), x.dtype),
        compiler_params=_cparams(("parallel", "parallel"), 32),
    )(x, g.reshape(1, d))


def _rope_tables(n, group, identity):
    if identity:
        return jnp.ones((n, HEAD_DIM), F32), jnp.zeros((n, HEAD_DIM), F32)
    p = jnp.arange(HEAD_DIM)
    half = 16 if group == "A" else 32
    use_col = ((p // (2 * half)) % 2) == 1
    first = (p % (2 * half)) < half
    freqs = ROPE_BASE ** (-(p % half).astype(F32) / half)
    t = jnp.arange(n)
    pos = jnp.where(use_col[None, :], (t % GRID_W)[:, None], (t // GRID_W)[:, None]).astype(F32)
    ang = pos * freqs[None, :]
    return jnp.cos(ang), jnp.where(first[None, :], -jnp.sin(ang), jnp.sin(ang))


def _rotate(x, cos, sin_signed, half):
    lane = lax.broadcasted_iota(jnp.int32, x.shape, 1)
    first = (lane % (2 * half)) < half
    partner = jnp.where(first, pltpu.roll(x, HEAD_DIM - half, 1), pltpu.roll(x, half, 1))
    return x * cos + partner * sin_signed


def _inproj_kernel(h_ref, w_ref, ca_ref, sa_ref, cb_ref, sb_ref, gq_ref, gk_ref, o_ref, wb_ref):
    j = pl.program_id(0)

    @pl.when(pl.program_id(1) == 0)
    def _():
        wb_ref[...] = w_ref[...].astype(BF16)

    acc = jnp.dot(h_ref[...], wb_ref[...], preferred_element_type=F32)
    nh = acc.shape[1] // HEAD_DIM
    heads = [slice(i * HEAD_DIM, (i + 1) * HEAD_DIM) for i in range(nh)]

    @pl.when(j < UNIT_AV)
    def _():
        scale = jnp.where(j < UNIT_AK, A_Q_SCALE, 1.0).astype(F32)
        for s in heads:
            o_ref[:, s] = (_rotate(acc[:, s], ca_ref[...], sa_ref[...], 16) * scale).astype(o_ref.dtype)

    @pl.when((j >= UNIT_BQ) & (j < UNIT_BV))
    def _():
        is_q = j < UNIT_BK
        g = jnp.where(is_q, gq_ref[...], gk_ref[...])
        scale = jnp.where(is_q, Q_SCALE, 1.0).astype(F32)
        for s in heads:
            y = _rms(acc[:, s], g)
            o_ref[:, s] = (_rotate(y, cb_ref[...], sb_ref[...], 32) * scale).astype(o_ref.dtype)

    @pl.when((j >= UNIT_CQ) & (j < UNIT_CK))
    def _():
        o_ref[...] = (acc * Q_SCALE).astype(o_ref.dtype)

    @pl.when(((j >= UNIT_AV) & (j < UNIT_BQ)) | (j == UNIT_BV) | (j >= UNIT_CK))
    def _():
        o_ref[...] = acc.astype(o_ref.dtype)


def _in_proj(h, w, tabs_a, tabs_b, gq, gk, n_per_sample):
    w_all, layer = w
    m, d = h.shape
    wid = w_all.shape[2]
    tn = wid // N_UNITS
    tm = min(1024, n_per_sample)
    tiles = n_per_sample // tm
    tab = pl.BlockSpec((tm, HEAD_DIM), lambda j, i: (i % tiles, 0))
    vec = pl.BlockSpec((1, HEAD_DIM), lambda j, i: (0, 0))
    return pl.pallas_call(
        _inproj_kernel,
        grid=(N_UNITS, m // tm),
        in_specs=[pl.BlockSpec((tm, d), lambda j, i: (i, 0)),
                  pl.BlockSpec((None, d, tn), lambda j, i: (layer, 0, j)),
                  tab, tab, tab, tab, vec, vec],
        out_specs=pl.BlockSpec((tm, tn), lambda j, i: (i, j)),
        out_shape=jax.ShapeDtypeStruct((m, wid), BF16),
        scratch_shapes=[pltpu.VMEM((d, tn), BF16)],
        compiler_params=_cparams(("arbitrary", "arbitrary"), 56),
    )(h, w_all, tabs_a[0], tabs_a[1], tabs_b[0], tabs_b[1], gq.reshape(1, HEAD_DIM), gk.reshape(1, HEAD_DIM))


ATTN_TILE_ROWS = 1024
ATTN_SUB_ROWS = 256


def _attn_kernel(*refs, diff, has_lat, group, lam_init):
    it = iter(refs)
    q_ref, kc_ref, vc_ref = next(it), next(it), next(it)
    kl_ref = vl_ref = lam_ref = g_ref = None
    if has_lat:
        kl_ref, vl_ref = next(it), next(it)
    if diff:
        lam_ref, g_ref = next(it), next(it)
    o_ref, k_scr, v_scr = next(it), next(it), next(it)
    n_ctx = kc_ref.shape[0]

    @pl.when(pl.program_id(2) == 0)
    def _():
        k_scr[0:n_ctx, :] = kc_ref[...]
        v_scr[0:n_ctx, 0:HEAD_DIM] = vc_ref[...]
        if has_lat:
            k_scr[n_ctx:, :] = kl_ref[...]
            v_scr[n_ctx:, 0:HEAD_DIM] = vl_ref[...]
        v_scr[:, HEAD_DIM:] = jnp.ones((v_scr.shape[0], HEAD_DIM), BF16)

    k, v = k_scr[...], v_scr[...]

    def softmax_pv(q):
        s = _nt_dot(q, k)
        p = jnp.exp2(s - jnp.max(s, axis=-1, keepdims=True)).astype(BF16)
        oe = jnp.dot(p, v, preferred_element_type=F32)
        return oe[:, :HEAD_DIM] * (1.0 / oe[:, HEAD_DIM:])

    sub = min(ATTN_SUB_ROWS, q_ref.shape[0])
    for r0 in range(0, q_ref.shape[0], sub):
        rows = slice(r0, r0 + sub)
        for g in range(group):
            cols = slice(g * HEAD_DIM, (g + 1) * HEAD_DIM)
            q = q_ref[rows, cols]
            if diff:
                qf = q.astype(F32)
                lane = lax.broadcasted_iota(jnp.int32, qf.shape, 1)
                o1 = softmax_pv(jnp.where(lane < A_QK_DIM, qf, 0.0).astype(BF16))
                o2 = softmax_pv(jnp.where(lane >= A_QK_DIM, qf, 0.0).astype(BF16))
                lv = lam_ref[...]
                lam = (jnp.exp(jnp.sum(lv[0:1] * lv[1:2], axis=-1, keepdims=True))
                       - jnp.exp(jnp.sum(lv[2:3] * lv[3:4], axis=-1, keepdims=True)) + lam_init)
                o = _rms(o1 - lam * o2, g_ref[...]) * (1.0 - lam_init)
            else:
                o = softmax_pv(q)
            o_ref[rows, cols] = o.astype(o_ref.dtype)


def _attention(pq, pc, px, *, n_q, n_ctx, n_lat, kv_heads, group, q_unit, k_unit, v_unit, unit_heads,
               diff=False, lam=None, subln_g=None, lam_init=0.0):
    bsz = pq.shape[0] // n_q
    has_lat = px is not None
    tq = min(ATTN_TILE_ROWS, n_q)
    nqt = n_q // tq
    gw = group * HEAD_DIM
    n_keys = n_ctx + (n_lat if has_lat else 0)
    qc, kc, vc = q_unit * unit_heads // group, k_unit * unit_heads, v_unit * unit_heads
    assert (q_unit * unit_heads) % group == 0
    in_specs = [pl.BlockSpec((tq, gw), lambda b, h, i: (b * nqt + i, qc + h)),
                pl.BlockSpec((n_ctx, HEAD_DIM), lambda b, h, i: (b, kc + h)),
                pl.BlockSpec((n_ctx, HEAD_DIM), lambda b, h, i: (b, vc + h))]
    args = [pq, pc, pc]
    if has_lat:
        in_specs += [pl.BlockSpec((n_lat, HEAD_DIM), lambda b, h, i: (b, kc + h)),
                     pl.BlockSpec((n_lat, HEAD_DIM), lambda b, h, i: (b, vc + h))]
        args += [px, px]
    if diff:
        in_specs += [pl.BlockSpec((4, A_QK_DIM), lambda b, h, i: (0, 0)),
                     pl.BlockSpec((1, HEAD_DIM), lambda b, h, i: (0, 0))]
        args += [lam, subln_g.reshape(1, HEAD_DIM)]
    return pl.pallas_call(
        functools.partial(_attn_kernel, diff=diff, has_lat=has_lat, group=group, lam_init=lam_init),
        grid=(bsz, kv_heads, nqt),
        in_specs=in_specs,
        out_specs=pl.BlockSpec((tq, gw), lambda b, h, i: (b * nqt + i, h)),
        out_shape=jax.ShapeDtypeStruct((bsz * n_q, kv_heads * gw), BF16),
        scratch_shapes=[pltpu.VMEM((n_keys, HEAD_DIM), BF16), pltpu.VMEM((n_keys, 2 * HEAD_DIM), BF16)],
        compiler_params=_cparams(("parallel", "parallel", "arbitrary"), 48),
    )(*args)


NA_QROWS = 4


def _na_plan(rows):
    kh = min(NA_ROWS, rows)
    qb = min(NA_QROWS, rows)
    assert rows % qb == 0
    kw = min(rows, kh + qb)
    starts, var_ids, variants = [], [], []
    for r0 in range(0, rows, qb):
        rs = [int(np.clip(r - kh // 2, 0, rows - kh)) for r in range(r0, r0 + qb)]
        kstart = min(min(rs), rows - kw)
        assert kstart >= 0 and max(rs) + kh <= kstart + kw
        sig = tuple((r0 + u - kstart, rs[u] - kstart) for u in range(qb))
        if sig not in variants:
            variants.append(sig)
        starts.append(kstart)
        var_ids.append(variants.index(sig))
    return kw, starts, var_ids, variants


def _na_toeplitz(c_rpb):
    q = np.arange(GRID_W)
    kc = np.arange(GRID_W)
    start = np.clip(q - NA_COLS // 2, 0, GRID_W - NA_COLS)
    inwin = (kc[None, :] >= start[:, None]) & (kc[None, :] < start[:, None] + NA_COLS)
    ci = kc[None, :] - q[:, None] + NA_COLS - 1
    onehot = ((ci[..., None] == np.arange(2 * NA_COLS - 1)) & inwin[..., None]).astype(np.float32)
    toep = jnp.einsum("lhab,qcb->lhaqc", c_rpb.astype(F32), jnp.asarray(onehot), precision=lax.Precision.HIGHEST)
    return jnp.where(jnp.asarray(inwin)[None, None, None], toep * LOG2E, NEG)


def _na_kernel(kst_ref, var_ref, q_ref, k_ref, v_ref, kc_ref, vc_ref, toep_ref, o_ref, t_ref, *,
               nblk, qlen, klen, variants, kh):
    kctx = kc_ref[...]
    vctx = vc_ref[...]

    masked = jnp.full((GRID_W, GRID_W), NEG, F32)
    for vi, sig in enumerate(variants):
        for u, (qoff, woff) in enumerate(sig):
            tile = lambda kr: toep_ref[kr - qoff + NA_ROWS - 1] if woff <= kr < woff + kh else masked
            for kr in range(0, klen // GRID_W, 2):
                t_ref[vi, u * GRID_W:(u + 1) * GRID_W, kr * GRID_W:(kr + 2) * GRID_W] = jnp.concatenate(
                    [tile(kr), tile(kr + 1)], axis=1)

    def body(rb, carry):
        qoff = pl.multiple_of(rb * qlen, qlen)
        koff = pl.multiple_of(kst_ref[rb] * GRID_W, GRID_W)
        q = q_ref[pl.ds(qoff, qlen), :]
        s_loc = _nt_dot(q, k_ref[pl.ds(koff, klen), :]) + t_ref[var_ref[rb]]
        s_ctx = _nt_dot(q, kctx)
        m = jnp.maximum(jnp.max(s_loc, axis=-1, keepdims=True), jnp.max(s_ctx, axis=-1, keepdims=True))
        p_loc = jnp.exp2(s_loc - m)
        p_ctx = jnp.exp2(s_ctx - m)
        l = jnp.sum(p_loc, axis=-1, keepdims=True) + jnp.sum(p_ctx, axis=-1, keepdims=True)
        o = (jnp.dot(p_loc.astype(BF16), v_ref[pl.ds(koff, klen), :], preferred_element_type=F32)
             + jnp.dot(p_ctx.astype(BF16), vctx, preferred_element_type=F32))
        o_ref[pl.ds(qoff, qlen), :] = (o * (1.0 / l)).astype(o_ref.dtype)
        return carry

    lax.fori_loop(0, nblk, body, 0, unroll=2 if nblk % 2 == 0 else 1)


def _neighbourhood(px, pc, toep, *, n_lat, n_ctx, heads, unit_heads):
    bsz = px.shape[0] // n_lat
    rows = n_lat // GRID_W
    kw, starts, var_ids, variants = _na_plan(rows)
    assert kw % 2 == 0
    nblk = len(starts)
    qlen, klen = n_lat // nblk, kw * GRID_W
    qc, kc, vc = UNIT_CQ * unit_heads, UNIT_CK * unit_heads, UNIT_CV * unit_heads
    lat = lambda c0: pl.BlockSpec((n_lat, HEAD_DIM), lambda b, h, ks, vi: (b, c0 + h))
    ctx = lambda c0: pl.BlockSpec((n_ctx, HEAD_DIM), lambda b, h, ks, vi: (b, c0 + h))
    return pl.pallas_call(
        functools.partial(_na_kernel, nblk=nblk, qlen=qlen, klen=klen, variants=tuple(variants),
                          kh=min(NA_ROWS, rows)),
        grid_spec=pltpu.PrefetchScalarGridSpec(
            num_scalar_prefetch=2,
            grid=(bsz, heads),
            in_specs=[lat(qc), lat(kc), lat(vc), ctx(kc), ctx(vc),
                      pl.BlockSpec((None,) + toep.shape[1:], lambda b, h, ks, vi: (h, 0, 0, 0))],
            out_specs=pl.BlockSpec((n_lat, HEAD_DIM), lambda b, h, ks, vi: (b, h)),
            scratch_shapes=[pltpu.VMEM((len(variants), qlen, klen), F32)]),
        out_shape=jax.ShapeDtypeStruct((bsz * n_lat, heads * HEAD_DIM), BF16),
        compiler_params=_cparams(("parallel", "parallel"), 40),
    )(jnp.asarray(starts, jnp.int32), jnp.asarray(var_ids, jnp.int32), px, px, px, pc, pc, toep)


def _outproj_kernel(oa_ref, ob_ref, oc_ref, w_ref, x_ref, g_ref, o_ref, wb_ref):
    @pl.when((pl.program_id(1) == 0) & (pl.program_id(2) == 0))
    def _():
        wb_ref[...] = w_ref[...].astype(BF16)

    wa, wb = oa_ref.shape[1], ob_ref.shape[1]
    acc = jnp.dot(oa_ref[...], wb_ref[0:wa, :], preferred_element_type=F32)
    acc += jnp.dot(ob_ref[...], wb_ref[wa:wa + wb, :], preferred_element_type=F32)
    acc += jnp.dot(oc_ref[...], wb_ref[wa + wb:, :], preferred_element_type=F32)
    o_ref[...] = x_ref[...] + g_ref[...] * acc


def _out_proj(oa, ob, oc, w, x, gate, row_fn):
    w_all, layer = w
    kdim = w_all.shape[1]
    bsz, n, d = x.shape
    tm = min(1024, n)
    tn = min(512, d)
    nt = n // tm
    lhs = lambda a: pl.BlockSpec((tm, a.shape[1]), lambda j, b, i: (b * nt + i, 0))
    return pl.pallas_call(
        _outproj_kernel,
        grid=(d // tn, bsz, nt),
        in_specs=[lhs(oa), lhs(ob), lhs(oc),
                  pl.BlockSpec((None, kdim, tn), lambda j, b, i: (layer, 0, j)),
                  pl.BlockSpec((None, tm, tn), lambda j, b, i: (b, i, j)),
                  pl.BlockSpec((None, 1, tn), lambda j, b, i: (row_fn(b), 0, j))],
        out_specs=pl.BlockSpec((None, tm, tn), lambda j, b, i: (b, i, j)),
        out_shape=jax.ShapeDtypeStruct((bsz, n, d), F32),
        scratch_shapes=[pltpu.VMEM((kdim, tn), BF16)],
        compiler_params=_cparams(("arbitrary", "arbitrary", "arbitrary"), 56),
    )(oa, ob, oc, w_all, x, gate)


def _split3(x):
    hi = x.astype(BF16)
    r = x - hi.astype(F32)
    mid = r.astype(BF16)
    lo = (r - mid.astype(F32)).astype(BF16)
    return hi, mid, lo


def _router_kernel(x_ref, g_ref, sh_ref, sc_ref, wh_ref, wl_ref, h_ref, afft_ref, aff_ref):
    y = _rms(x_ref[...], g_ref[...])
    h = y * (1.0 + sc_ref[...]) + sh_ref[...]
    h_ref[...] = h
    hh = h.astype(BF16)
    hl = (h - hh.astype(F32)).astype(BF16)
    wh, wl = wh_ref[...], wl_ref[...]
    lt = _nt_dot(wh, hh) + _nt_dot(wh, hl) + _nt_dot(wl, hh)
    e = jnp.exp(lt - jnp.max(lt, axis=0, keepdims=True))
    afft_ref[...] = e / jnp.sum(e, axis=0, keepdims=True)
    ln = _nt_dot(hh, wh) + _nt_dot(hl, wh) + _nt_dot(hh, wl)
    en = jnp.exp(ln - jnp.max(ln, axis=-1, keepdims=True))
    aff_ref[...] = en / jnp.sum(en, axis=-1, keepdims=True)


def _router(x, g, shift, scale, row_fn, w_router):
    bsz, n, d = x.shape
    ne = w_router.shape[1]
    tm = min(256, n)
    wt = w_router.T
    wh = wt.astype(BF16)
    wl = (wt - wh.astype(F32)).astype(BF16)
    wspec = pl.BlockSpec((ne, d), lambda b, i: (0, 0))
    return pl.pallas_call(
        _router_kernel,
        grid=(bsz, n // tm),
        in_specs=[pl.BlockSpec((None, tm, d), lambda b, i: (b, i, 0)),
                  pl.BlockSpec((1, d), lambda b, i: (0, 0)),
                  _row_spec(d, row_fn), _row_spec(d, row_fn), wspec, wspec],
        out_specs=[pl.BlockSpec((None, tm, d), lambda b, i: (b, i, 0)),
                   pl.BlockSpec((None, ne, tm), lambda b, i: (b, 0, i)),
                   pl.BlockSpec((None, tm, ne), lambda b, i: (b, i, 0))],
        out_shape=[jax.ShapeDtypeStruct((bsz, n, d), F32),
                   jax.ShapeDtypeStruct((bsz, ne, n), F32),
                   jax.ShapeDtypeStruct((bsz, n, ne), F32)],
        compiler_params=_cparams(("parallel", "parallel"), 40),
    )(x, g.reshape(1, d), shift, scale, wh, wl)


def _lane_cumsum(x01, tri):
    r, n = x01.shape
    carry = jnp.zeros((r, 1), F32)
    out = []
    for c in range(n // HEAD_DIM):
        blk = x01[:, c * HEAD_DIM:(c + 1) * HEAD_DIM].astype(BF16)
        cs = jnp.dot(blk, tri, preferred_element_type=F32) + carry
        out.append(cs)
        carry = cs[:, HEAD_DIM - 1:HEAD_DIM]
    return jnp.concatenate(out, axis=1) if len(out) > 1 else out[0]


def _select_kernel(afft_ref, r_ref, o_ref, *, cap):
    a = afft_ref[...]
    ne, n = a.shape
    bits = pltpu.bitcast(a, jnp.int32)

    def bisect(_, lohi):
        lo, hi = lohi
        mid = lo + lax.shift_right_logical(hi - lo, 1)
        cnt = jnp.sum((bits >= mid).astype(F32), axis=1, keepdims=True)
        ok = cnt >= cap
        return jnp.where(ok, mid, lo), jnp.where(ok, hi, mid)

    lo0 = jnp.zeros((ne, 1), jnp.int32)
    hi0 = jnp.full((ne, 1), 0x7F800000, jnp.int32)
    thr, _ = lax.fori_loop(0, 32, bisect, (lo0, hi0))

    ri = lax.broadcasted_iota(jnp.int32, (HEAD_DIM, HEAD_DIM), 0)
    ci = lax.broadcasted_iota(jnp.int32, (HEAD_DIM, HEAD_DIM), 1)
    tri = (ri <= ci).astype(BF16)
    gt = (bits > thr).astype(F32)
    eq = (bits == thr).astype(F32)
    need = cap - jnp.sum(gt, axis=1, keepdims=True)
    sel = gt + eq * (_lane_cumsum(eq, tri) <= need).astype(F32)
    pos = _lane_cumsum(sel, tri) - 1.0

    slot = lax.broadcasted_iota(jnp.int32, (cap, n), 0).astype(F32)
    rmat = r_ref[...]
    for e in range(ne):
        onehot = jnp.where((pos[e:e + 1, :] == slot) & (sel[e:e + 1, :] > 0.5), 1.0, 0.0).astype(BF16)
        o_ref[e] = jnp.dot(onehot, rmat, preferred_element_type=F32)


def _select(afft, aff, cap):
    bsz, ne, n = afft.shape
    t = jnp.arange(n)
    hi, mid, lo = _split3(aff)
    tcols = jnp.stack([t // 64, t % 64], axis=1).astype(BF16)
    pad = jnp.zeros((bsz, n, HEAD_DIM - 2 - 3 * ne), BF16)
    rmat = jnp.concatenate([jnp.broadcast_to(tcols[None], (bsz, n, 2)), hi, mid, lo, pad], axis=2)
    tab = pl.pallas_call(
        functools.partial(_select_kernel, cap=cap),
        grid=(bsz,),
        in_specs=[pl.BlockSpec((None, ne, n), lambda b: (b, 0, 0)),
                  pl.BlockSpec((None, n, HEAD_DIM), lambda b: (b, 0, 0))],
        out_specs=pl.BlockSpec((None, ne, cap, HEAD_DIM), lambda b: (b, 0, 0, 0)),
        out_shape=jax.ShapeDtypeStruct((bsz, ne, cap, HEAD_DIM), F32),
        compiler_params=_cparams(("parallel",), 40),
    )(afft, rmat)
    idx = (tab[..., 0] * 64.0 + tab[..., 1]).astype(jnp.int32)
    parts = tab[..., 2:2 + 3 * ne].reshape(bsz, ne, cap, 3, ne).sum(axis=3)
    gates = (parts * jnp.eye(ne, dtype=F32)[None, :, None, :]).sum(axis=-1, keepdims=True)
    return idx, gates


def _ffn_up_kernel(idx_ref, h_hbm, wg_ref, wu_ref, o_ref, xg_ref, sem, *, cap, ne, nb):
    e, b = pl.program_id(0), pl.program_id(1)
    step = e * nb + b
    slot = lax.rem(step, 2)

    def gather(e_, b_, slot_):
        base = (b_ * ne + e_) * cap
        for j in range(cap):
            pltpu.make_async_copy(h_hbm.at[b_, pl.ds(idx_ref[base + j], 1), :],
                                  xg_ref.at[slot_, pl.ds(j, 1), :], sem.at[slot_]).start()

    @pl.when(step == 0)
    def _():
        gather(e, b, 0)

    pltpu.make_async_copy(h_hbm.at[b, pl.ds(0, cap), :], xg_ref.at[slot], sem.at[slot]).wait()

    last = step + 1 >= ne * nb
    nxt = jnp.where(last, step, step + 1)
    gather(lax.div(nxt, nb), lax.rem(nxt, nb), 1 - slot)

    x = xg_ref[slot].astype(BF16)
    gate = jnp.dot(x, wg_ref[...].astype(BF16), preferred_element_type=F32)
    up = jnp.dot(x, wu_ref[...].astype(BF16), preferred_element_type=F32)
    o_ref[...] = (gate * jax.nn.sigmoid(gate) * up).astype(o_ref.dtype)

    @pl.when(last)
    def _():
        pltpu.make_async_copy(h_hbm.at[b, pl.ds(0, cap), :], xg_ref.at[1 - slot], sem.at[1 - slot]).wait()


def _ffn_up(idx_flat, h, wg, wu, cap):
    bsz, n, d = h.shape
    (wg_all, layer), (wu_all, _) = wg, wu
    ne, _, ff = wg_all.shape[1:]
    wspec = pl.BlockSpec((None, None, d, ff), lambda e, b, idx: (layer, e, 0, 0))
    return pl.pallas_call(
        functools.partial(_ffn_up_kernel, cap=cap, ne=ne, nb=bsz),
        grid_spec=pltpu.PrefetchScalarGridSpec(
            num_scalar_prefetch=1,
            grid=(ne, bsz),
            in_specs=[pl.BlockSpec(memory_space=pl.ANY), wspec, wspec],
            out_specs=pl.BlockSpec((None, None, cap, ff), lambda e, b, idx: (b, e, 0, 0)),
            scratch_shapes=[pltpu.VMEM((2, cap, d), F32), pltpu.SemaphoreType.DMA((2,))]),
        out_shape=jax.ShapeDtypeStruct((bsz, ne, cap, ff), BF16),
        compiler_params=_cparams(("arbitrary", "arbitrary"), 58),
    )(idx_flat, h, wg_all, wu_all)


ACC_FULL_WIDTH_BYTES = 8 * MIB
SCATTER_ROWS = 4


def _ffn_down_kernel(idx_ref, hid_ref, wd_ref, gate_ref, x_ref, g2_ref, o_ref, acc_ref, ye_ref, *, cap, ne, tr):
    b, e = pl.program_id(0), pl.program_id(2)

    @pl.when(e == 0)
    def _():
        acc_ref[...] = jnp.zeros_like(acc_ref)

    @pl.when(e < ne)
    def _():
        ye_ref[...] = jnp.dot(hid_ref[...], wd_ref[...].astype(BF16), preferred_element_type=F32) * gate_ref[...]
        base = (b * ne + e) * cap

        for j0 in range(0, cap, SCATTER_ROWS):
            toks = [idx_ref[base + j0 + u] for u in range(SCATTER_ROWS)]
            rows = [acc_ref[pl.ds(t, 1), :] for t in toks]
            for u, (t, r) in enumerate(zip(toks, rows)):
                acc_ref[pl.ds(t, 1), :] = r + ye_ref[j0 + u:j0 + u + 1, :]

    @pl.when(e >= ne)
    def _():
        r0 = pl.multiple_of((e - ne) * tr, tr)
        o_ref[...] = x_ref[...] + g2_ref[...] * acc_ref[pl.ds(r0, tr), :]


def _ffn_down(idx_flat, hid, wd, gates, x, g2, row_fn):
    bsz, n, d = x.shape
    wd_all, layer = wd
    ne, ff = wd_all.shape[1:3]
    cap = hid.shape[2]
    assert cap % SCATTER_ROWS == 0
    cb = d if n * d * 4 <= ACC_FULL_WIDTH_BYTES else min(1024, d)
    tr = min(512, n)
    ex = lambda e: jnp.minimum(e, ne - 1)
    rt = lambda e: jnp.maximum(e - ne, 0)
    return pl.pallas_call(
        functools.partial(_ffn_down_kernel, cap=cap, ne=ne, tr=tr),
        grid_spec=pltpu.PrefetchScalarGridSpec(
            num_scalar_prefetch=1,
            grid=(bsz, d // cb, ne + n // tr),
            in_specs=[pl.BlockSpec((None, None, cap, ff), lambda b, c, e, idx: (b, ex(e), 0, 0)),
                      pl.BlockSpec((None, None, ff, cb), lambda b, c, e, idx: (layer, ex(e), 0, c)),
                      pl.BlockSpec((None, None, cap, 1), lambda b, c, e, idx: (b, ex(e), 0, 0)),
                      pl.BlockSpec((None, tr, cb), lambda b, c, e, idx: (b, rt(e), c)),
                      pl.BlockSpec((None, 1, cb), lambda b, c, e, idx: (row_fn(b), 0, c))],
            out_specs=pl.BlockSpec((None, tr, cb), lambda b, c, e, idx: (b, rt(e), c)),
            scratch_shapes=[pltpu.VMEM((n, cb), F32), pltpu.VMEM((cap, cb), F32)]),
        out_shape=jax.ShapeDtypeStruct((bsz, n, d), F32),
        compiler_params=_cparams(("parallel", "parallel", "arbitrary"), 48),
    )(idx_flat, hid, wd_all, gates, x, g2)


def _expert_ffn_residual(x, norm_g, shift, scale, g2, row_fn, w_router, wg, wu, wd):
    n = x.shape[1]
    ne = w_router.shape[1]
    cap = EC_CAPACITY * n // ne
    h, afft, aff = _router(x, norm_g, shift, scale, row_fn, w_router)
    idx, gates = _select(afft, aff, cap)
    idx_flat = idx.reshape(-1)
    hid = _ffn_up(idx_flat, h, wg, wu, cap)
    return _ffn_down(idx_flat, hid, wd, gates, x, g2, row_fn)


def kernel(x, c, ctx, c_ctx, w_ada, b_ada, norm1_g, norm2_g, w_in, w_out, a_lambda, a_subln_g, b_q_norm_g,
           b_k_norm_g, c_rpb, w_router, w_e_gate, w_e_up, w_e_down, final_g):
    bsz, n, d = x.shape
    n_ctx = ctx.shape[1]
    depth = w_ada.shape[0]
    rows = n // GRID_W
    unit_heads = d // 8 // HEAD_DIM
    a_heads, b_heads, c_heads = 2 * unit_heads, 3 * unit_heads, 3 * unit_heads
    assert bsz + 1 <= MOD_ROWS and rows >= 1 and n % GRID_W == 0

    cvec = jnp.zeros((MOD_ROWS, d), F32).at[:bsz].set(c).at[bsz].set(c_ctx)
    mod = _ada_mod(cvec, w_ada, b_ada)
    lat_row = lambda b: b
    ctx_row = lambda b: bsz

    tabs_a, tabs_b = _rope_tables(n, "A", False), _rope_tables(n, "B", False)
    tabs_id = _rope_tables(n_ctx, "A", True)
    na_toep = _na_toeplitz(c_rpb)

    for l in range(depth):
        need_ctx = l < depth - 1
        lam_init = 0.8 - 0.6 * float(np.exp(-0.3 * l))
        m6 = [mod[l, :, k * d:(k + 1) * d].reshape(MOD_ROWS, 1, d) for k in range(N_MOD)]
        sh1, sc1, g1, sh2, sc2, g2 = m6
        w_in_l, w_out_l = (w_in, l), (w_out, l)
        wg, wu, wd = (w_e_gate, l), (w_e_up, l), (w_e_down, l)

        hx = _norm_mod(x, norm1_g[l], sh1, sc1, lat_row, BF16).reshape(bsz * n, d)
        hc = _norm_mod(ctx, norm1_g[l], sh1, sc1, ctx_row, BF16).reshape(bsz * n_ctx, d)
        px = _in_proj(hx, w_in_l, tabs_a, tabs_b, b_q_norm_g[l], b_k_norm_g[l], n)
        pc = _in_proj(hc, w_in_l, tabs_id, tabs_id, b_q_norm_g[l], b_k_norm_g[l], n_ctx)

        common = dict(n_ctx=n_ctx, n_lat=n, unit_heads=unit_heads)
        a_args = dict(kv_heads=a_heads, group=1, q_unit=UNIT_AQ, k_unit=UNIT_AK, v_unit=UNIT_AV, diff=True,
                      lam=a_lambda[l], subln_g=a_subln_g[l], lam_init=lam_init)
        b_args = dict(kv_heads=unit_heads, group=b_heads // unit_heads, q_unit=UNIT_BQ, k_unit=UNIT_BK, v_unit=UNIT_BV)
        oa = _attention(px, pc, px, n_q=n, **a_args, **common)
        ob = _attention(px, pc, px, n_q=n, **b_args, **common)
        oc = _neighbourhood(px, pc, na_toep[l], n_lat=n, n_ctx=n_ctx, heads=c_heads, unit_heads=unit_heads)
        x = _out_proj(oa, ob, oc, w_out_l, x, g1, lat_row)
        x = _expert_ffn_residual(x, norm2_g[l], sh2, sc2, g2, lat_row, w_router[l], wg, wu, wd)

        if need_ctx:
            c_args = dict(kv_heads=c_heads, group=1, q_unit=UNIT_CQ, k_unit=UNIT_CK, v_unit=UNIT_CV)
            oa_c = _attention(pc, pc, None, n_q=n_ctx, **a_args, **common)
            ob_c = _attention(pc, pc, None, n_q=n_ctx, **b_args, **common)
            oc_c = _attention(pc, pc, None, n_q=n_ctx, **c_args, **common)
            ctx = _out_proj(oa_c, ob_c, oc_c, w_out_l, ctx, g1, ctx_row)
            ctx = _expert_ffn_residual(ctx, norm2_g[l], sh2, sc2, g2, ctx_row, w_router[l], wg, wu, wd)

    return _final_norm(x, final_g)
```

```python
import functools

import numpy as np
import jax
import jax.numpy as jnp
from jax import lax
from jax.experimental import pallas as pl
from jax.experimental.pallas import tpu as pltpu

HEAD_DIM = 128
A_QK_DIM = HEAD_DIM // 2
GRID_W = 64
NA_ROWS = 8
NA_COLS = 16
EC_CAPACITY = 2
N_MOD = 6
ROPE_BASE = 10000.0
EPS = 1e-6
NEG = -1e30
LOG2E = 1.4426950408889634
A_Q_SCALE = A_QK_DIM ** -0.5 * LOG2E
Q_SCALE = HEAD_DIM ** -0.5 * LOG2E
MIB = 1024 * 1024
MOD_ROWS = 8
F32 = jnp.float32
BF16 = jnp.bfloat16

UNIT_AQ, UNIT_AK, UNIT_AV = 0, 2, 4
UNIT_BQ, UNIT_BK, UNIT_BV = 6, 9, 10
UNIT_CQ, UNIT_CK, UNIT_CV = 11, 14, 17
N_UNITS = 20


def _cparams(sem, vmem_mib):
    return pltpu.CompilerParams(dimension_semantics=sem, vmem_limit_bytes=vmem_mib * MIB)


def _nt_dot(a, b):
    return lax.dot_general(a, b, (((1,), (1,)), ((), ())), preferred_element_type=F32)


def _ada_kernel(c_ref, w_ref, b_ref, o_ref):
    c = c_ref[...]
    s = (c * jax.nn.sigmoid(c)).astype(BF16)
    o_ref[...] = jnp.dot(s, w_ref[...].astype(BF16), preferred_element_type=F32) + b_ref[...]


def _ada_mod(cvec, w_ada, b_ada):
    depth, d, nm = w_ada.shape
    tn = min(512, nm)
    return pl.pallas_call(
        _ada_kernel,
        grid=(depth, nm // tn),
        in_specs=[pl.BlockSpec((MOD_ROWS, d), lambda l, j: (0, 0)),
                  pl.BlockSpec((None, d, tn), lambda l, j: (l, 0, j)),
                  pl.BlockSpec((None, 1, tn), lambda l, j: (l, 0, j))],
        out_specs=pl.BlockSpec((None, MOD_ROWS, tn), lambda l, j: (l, 0, j)),
        out_shape=jax.ShapeDtypeStruct((depth, MOD_ROWS, nm), F32),
        compiler_params=_cparams(("parallel", "parallel"), 40),
    )(cvec, w_ada, b_ada.reshape(depth, 1, nm))


def _rms(x, g):
    return x * lax.rsqrt(jnp.mean(x * x, axis=-1, keepdims=True) + EPS) * g


def _norm_mod_kernel(x_ref, g_ref, sh_ref, sc_ref, o_ref):
    y = _rms(x_ref[...], g_ref[...])
    o_ref[...] = (y * (1.0 + sc_ref[...]) + sh_ref[...]).astype(o_ref.dtype)


def _norm_kernel(x_ref, g_ref, o_ref):
    o_ref[...] = _rms(x_ref[...], g_ref[...]).astype(o_ref.dtype)


def _row_spec(d, row_fn):
    return pl.BlockSpec((None, 1, d), lambda b, i: (row_fn(b), 0, 0))


def _norm_mod(x, g, shift, scale, row_fn, out_dtype):
    bsz, n, d = x.shape
    tm = min(256, n)
    return pl.pallas_call(
        _norm_mod_kernel,
        grid=(bsz, n // tm),
        in_specs=[pl.BlockSpec((None, tm, d), lambda b, i: (b, i, 0)),
                  pl.BlockSpec((1, d), lambda b, i: (0, 0)),
                  _row_spec(d, row_fn), _row_spec(d, row_fn)],
        out_specs=pl.BlockSpec((None, tm, d), lambda b, i: (b, i, 0)),
        out_shape=jax.ShapeDtypeStruct((bsz, n, d), out_dtype),
        compiler_params=_cparams(("parallel", "parallel"), 32),
    )(x, g.reshape(1, d), shift, scale)


def _final_norm(x, g):
    bsz, n, d = x.shape
    tm = min(256, n)
    return pl.pallas_call(
        _norm_kernel,
        grid=(bsz, n // tm),
        in_specs=[pl.BlockSpec((None, tm, d), lambda b, i: (b, i, 0)),
                  pl.BlockSpec((1, d), lambda b, i: (0, 0))],
        out_specs=pl.BlockSpec((None, tm, d), lambda b, i: (b, i, 0)),
        out_shape=jax.ShapeDtypeStruct((bsz, n, d), x.dtype),
        compiler_params=_cparams(("parallel", "parallel"), 32),
    )(x, g.reshape(1, d))


def _rope_tables(n, group):
    p = jnp.arange(HEAD_DIM)
    half = 16 if group == "A" else 32
    use_col = ((p // (2 * half)) % 2) == 1
    first = (p % (2 * half)) < half
    freqs = ROPE_BASE ** (-(p % half).astype(F32) / half)
    t = jnp.arange(n)
    pos = jnp.where(use_col[None, :], (t % GRID_W)[:, None], (t // GRID_W)[:, None]).astype(F32)
    ang = pos * freqs[None, :]
    return jnp.cos(ang), jnp.where(first[None, :], -jnp.sin(ang), jnp.sin(ang))


def _rotate(x, cos, sin_signed, half):
    lane = lax.broadcasted_iota(jnp.int32, x.shape, 1)
    first = (lane % (2 * half)) < half
    partner = jnp.where(first, pltpu.roll(x, HEAD_DIM - half, 1), pltpu.roll(x, half, 1))
    return x * cos + partner * sin_signed


def _inproj_emit(j, h, wb_ref, rope_a, rope_b, gq_ref, gk_ref, o_ref):
    acc = jnp.dot(h, wb_ref[...], preferred_element_type=F32)
    nh = acc.shape[1] // HEAD_DIM
    heads = [slice(i * HEAD_DIM, (i + 1) * HEAD_DIM) for i in range(nh)]
    rot = lambda x, tabs, half: x if tabs is None else _rotate(x, tabs[0][...], tabs[1][...], half)

    @pl.when(j < UNIT_AV)
    def _():
        scale = jnp.where(j < UNIT_AK, A_Q_SCALE, 1.0).astype(F32)
        for s in heads:
            o_ref[:, s] = (rot(acc[:, s], rope_a, 16) * scale).astype(o_ref.dtype)

    @pl.when((j >= UNIT_BQ) & (j < UNIT_BV))
    def _():
        is_q = j < UNIT_BK
        g = jnp.where(is_q, gq_ref[...], gk_ref[...])
        scale = jnp.where(is_q, Q_SCALE, 1.0).astype(F32)
        for s in heads:
            o_ref[:, s] = (rot(_rms(acc[:, s], g), rope_b, 32) * scale).astype(o_ref.dtype)

    @pl.when((j >= UNIT_CQ) & (j < UNIT_CK))
    def _():
        o_ref[...] = (acc * Q_SCALE).astype(o_ref.dtype)

    @pl.when(((j >= UNIT_AV) & (j < UNIT_BQ)) | (j == UNIT_BV) | (j >= UNIT_CK))
    def _():
        o_ref[...] = acc.astype(o_ref.dtype)


def _inproj_kernel(hx_ref, hc_ref, w_ref, ca_ref, sa_ref, cb_ref, sb_ref, gq_ref, gk_ref, ox_ref, oc_ref, wb_ref, *, nt):
    j, i = pl.program_id(0), pl.program_id(1)

    @pl.when(i == 0)
    def _():
        wb_ref[...] = w_ref[...].astype(BF16)

    @pl.when(i < nt)
    def _():
        _inproj_emit(j, hx_ref[...], wb_ref, (ca_ref, sa_ref), (cb_ref, sb_ref), gq_ref, gk_ref, ox_ref)

    @pl.when(i == nt)
    def _():
        _inproj_emit(j, hc_ref[...], wb_ref, None, None, gq_ref, gk_ref, oc_ref)


def _in_proj(hx, hc, w, tabs_a, tabs_b, gq, gk, n_per_sample):
    w_all, layer = w
    m, d = hx.shape
    mc = hc.shape[0]
    wid = w_all.shape[2]
    tn = wid // N_UNITS
    tm = min(1024, n_per_sample)
    tiles = n_per_sample // tm
    nt = m // tm
    lat = lambda i: jnp.minimum(i, nt - 1)
    tab = pl.BlockSpec((tm, HEAD_DIM), lambda j, i: (lat(i) % tiles, 0))
    vec = pl.BlockSpec((1, HEAD_DIM), lambda j, i: (0, 0))
    return pl.pallas_call(
        functools.partial(_inproj_kernel, nt=nt),
        grid=(N_UNITS, nt + 1),
        in_specs=[pl.BlockSpec((tm, d), lambda j, i: (lat(i), 0)),
                  pl.BlockSpec((mc, d), lambda j, i: (0, 0)),
                  pl.BlockSpec((None, d, tn), lambda j, i: (layer, 0, j)),
                  tab, tab, tab, tab, vec, vec],
        out_specs=[pl.BlockSpec((tm, tn), lambda j, i: (lat(i), j)),
                   pl.BlockSpec((mc, tn), lambda j, i: (0, j))],
        out_shape=[jax.ShapeDtypeStruct((m, wid), BF16), jax.ShapeDtypeStruct((mc, wid), BF16)],
        scratch_shapes=[pltpu.VMEM((d, tn), BF16)],
        compiler_params=_cparams(("arbitrary", "arbitrary"), 56),
    )(hx, hc, w_all, tabs_a[0], tabs_a[1], tabs_b[0], tabs_b[1], gq.reshape(1, HEAD_DIM), gk.reshape(1, HEAD_DIM))


ATTN_TILE_ROWS = 1024
ATTN_SUB_ROWS = 256


def _attn_kernel(*refs, diff, has_lat, group, lam_init):
    it = iter(refs)
    q_ref, kc_ref, vc_ref = next(it), next(it), next(it)
    kl_ref = vl_ref = lam_ref = g_ref = None
    if has_lat:
        kl_ref, vl_ref = next(it), next(it)
    if diff:
        lam_ref, g_ref = next(it), next(it)
    o_ref, k_scr, v_scr = next(it), next(it), next(it)
    n_ctx = kc_ref.shape[0]

    @pl.when(pl.program_id(2) == 0)
    def _():
        k_scr[0:n_ctx, :] = kc_ref[...]
        v_scr[0:n_ctx, 0:HEAD_DIM] = vc_ref[...]
        if has_lat:
            k_scr[n_ctx:, :] = kl_ref[...]
            v_scr[n_ctx:, 0:HEAD_DIM] = vl_ref[...]
        v_scr[:, HEAD_DIM:] = jnp.ones((v_scr.shape[0], HEAD_DIM), BF16)

    k, v = k_scr[...], v_scr[...]

    def softmax_pv(q):
        s = _nt_dot(q, k)
        p = jnp.exp2(s - jnp.max(s, axis=-1, keepdims=True)).astype(BF16)
        oe = jnp.dot(p, v, preferred_element_type=F32)
        return oe[:, :HEAD_DIM] * (1.0 / oe[:, HEAD_DIM:])

    sub = min(ATTN_SUB_ROWS, q_ref.shape[0])
    for r0 in range(0, q_ref.shape[0], sub):
        rows = slice(r0, r0 + sub)
        for g in range(group):
            cols = slice(g * HEAD_DIM, (g + 1) * HEAD_DIM)
            q = q_ref[rows, cols]
            if diff:
                qf = q.astype(F32)
                lane = lax.broadcasted_iota(jnp.int32, qf.shape, 1)
                o1 = softmax_pv(jnp.where(lane < A_QK_DIM, qf, 0.0).astype(BF16))
                o2 = softmax_pv(jnp.where(lane >= A_QK_DIM, qf, 0.0).astype(BF16))
                lv = lam_ref[...]
                lam = (jnp.exp(jnp.sum(lv[0:1] * lv[1:2], axis=-1, keepdims=True))
                       - jnp.exp(jnp.sum(lv[2:3] * lv[3:4], axis=-1, keepdims=True)) + lam_init)
                o = _rms(o1 - lam * o2, g_ref[...]) * (1.0 - lam_init)
            else:
                o = softmax_pv(q)
            o_ref[rows, cols] = o.astype(o_ref.dtype)


def _attention(pq, pc, px, *, n_q, n_ctx, n_lat, kv_heads, group, q_unit, k_unit, v_unit, unit_heads,
               diff=False, lam=None, subln_g=None, lam_init=0.0):
    bsz = pq.shape[0] // n_q
    has_lat = px is not None
    tq = min(ATTN_TILE_ROWS, n_q)
    nqt = n_q // tq
    gw = group * HEAD_DIM
    n_keys = n_ctx + (n_lat if has_lat else 0)
    qc, kc, vc = q_unit * unit_heads // group, k_unit * unit_heads, v_unit * unit_heads
    assert (q_unit * unit_heads) % group == 0
    in_specs = [pl.BlockSpec((tq, gw), lambda b, h, i: (b * nqt + i, qc + h)),
                pl.BlockSpec((n_ctx, HEAD_DIM), lambda b, h, i: (b, kc + h)),
                pl.BlockSpec((n_ctx, HEAD_DIM), lambda b, h, i: (b, vc + h))]
    args = [pq, pc, pc]
    if has_lat:
        in_specs += [pl.BlockSpec((n_lat, HEAD_DIM), lambda b, h, i: (b, kc + h)),
                     pl.BlockSpec((n_lat, HEAD_DIM), lambda b, h, i: (b, vc + h))]
        args += [px, px]
    if diff:
        in_specs += [pl.BlockSpec((4, A_QK_DIM), lambda b, h, i: (0, 0)),
                     pl.BlockSpec((1, HEAD_DIM), lambda b, h, i: (0, 0))]
        args += [lam, subln_g.reshape(1, HEAD_DIM)]
    return pl.pallas_call(
        functools.partial(_attn_kernel, diff=diff, has_lat=has_lat, group=group, lam_init=lam_init),
        grid=(bsz, kv_heads, nqt),
        in_specs=in_specs,
        out_specs=pl.BlockSpec((tq, gw), lambda b, h, i: (b * nqt + i, h)),
        out_shape=jax.ShapeDtypeStruct((bsz * n_q, kv_heads * gw), BF16),
        scratch_shapes=[pltpu.VMEM((n_keys, HEAD_DIM), BF16), pltpu.VMEM((n_keys, 2 * HEAD_DIM), BF16)],
        compiler_params=_cparams(("parallel", "parallel", "arbitrary"), 48),
    )(*args)


NA_QROWS = 4


def _na_plan(rows):
    kh = min(NA_ROWS, rows)
    qb = min(NA_QROWS, rows)
    assert rows % qb == 0
    kw = min(rows, kh + qb)
    starts, var_ids, variants = [], [], []
    for r0 in range(0, rows, qb):
        rs = [int(np.clip(r - kh // 2, 0, rows - kh)) for r in range(r0, r0 + qb)]
        kstart = min(min(rs), rows - kw)
        assert kstart >= 0 and max(rs) + kh <= kstart + kw
        sig = tuple((r0 + u - kstart, rs[u] - kstart) for u in range(qb))
        if sig not in variants:
            variants.append(sig)
        starts.append(kstart)
        var_ids.append(variants.index(sig))
    return kw, starts, var_ids, variants


def _na_toeplitz(c_rpb):
    q = np.arange(GRID_W)
    kc = np.arange(GRID_W)
    start = np.clip(q - NA_COLS // 2, 0, GRID_W - NA_COLS)
    inwin = (kc[None, :] >= start[:, None]) & (kc[None, :] < start[:, None] + NA_COLS)
    ci = kc[None, :] - q[:, None] + NA_COLS - 1
    onehot = ((ci[..., None] == np.arange(2 * NA_COLS - 1)) & inwin[..., None]).astype(np.float32)
    toep = jnp.einsum("lhab,qcb->lhaqc", c_rpb.astype(F32), jnp.asarray(onehot), precision=lax.Precision.HIGHEST)
    return jnp.where(jnp.asarray(inwin)[None, None, None], toep * LOG2E, NEG)


def _na_kernel(kst_ref, var_ref, q_ref, k_ref, v_ref, kc_ref, vc_ref, toep_ref, o_ref, t_ref, va_ref, vca_ref, *,
               nblk, qlen, klen, variants, kh):
    kctx = kc_ref[...]
    va_ref[:, 0:HEAD_DIM] = v_ref[...]
    va_ref[:, HEAD_DIM:] = jnp.ones((va_ref.shape[0], HEAD_DIM), BF16)
    vca_ref[:, 0:HEAD_DIM] = vc_ref[...]
    vca_ref[:, HEAD_DIM:] = jnp.ones((vca_ref.shape[0], HEAD_DIM), BF16)
    vctx = vca_ref[...]

    masked = jnp.full((GRID_W, GRID_W), NEG, F32)
    for vi, sig in enumerate(variants):
        for u, (qoff, woff) in enumerate(sig):
            tile = lambda kr: toep_ref[kr - qoff + NA_ROWS - 1] if woff <= kr < woff + kh else masked
            for kr in range(0, klen // GRID_W, 2):
                t_ref[vi, u * GRID_W:(u + 1) * GRID_W, kr * GRID_W:(kr + 2) * GRID_W] = jnp.concatenate(
                    [tile(kr), tile(kr + 1)], axis=1)

    def body(rb, carry):
        qoff = pl.multiple_of(rb * qlen, qlen)
        koff = pl.multiple_of(kst_ref[rb] * GRID_W, GRID_W)
        q = q_ref[pl.ds(qoff, qlen), :]
        s_loc = _nt_dot(q, k_ref[pl.ds(koff, klen), :]) + t_ref[var_ref[rb]]
        s_ctx = _nt_dot(q, kctx)
        m = jnp.maximum(jnp.max(s_loc, axis=-1, keepdims=True), jnp.max(s_ctx, axis=-1, keepdims=True))
        p_loc = jnp.exp2(s_loc - m).astype(BF16)
        p_ctx = jnp.exp2(s_ctx - m).astype(BF16)
        oe = (jnp.dot(p_loc, va_ref[pl.ds(koff, klen), :], preferred_element_type=F32)
              + jnp.dot(p_ctx, vctx, preferred_element_type=F32))
        o_ref[pl.ds(qoff, qlen), :] = (oe[:, :HEAD_DIM] * (1.0 / oe[:, HEAD_DIM:])).astype(o_ref.dtype)
        return carry

    lax.fori_loop(0, nblk, body, 0, unroll=4 if nblk % 4 == 0 else 1)


def _neighbourhood(px, pc, toep, *, n_lat, n_ctx, heads, unit_heads):
    bsz = px.shape[0] // n_lat
    rows = n_lat // GRID_W
    kw, starts, var_ids, variants = _na_plan(rows)
    assert kw % 2 == 0
    nblk = len(starts)
    qlen, klen = n_lat // nblk, kw * GRID_W
    qc, kc, vc = UNIT_CQ * unit_heads, UNIT_CK * unit_heads, UNIT_CV * unit_heads
    lat = lambda c0: pl.BlockSpec((n_lat, HEAD_DIM), lambda b, h, ks, vi: (b, c0 + h))
    ctx = lambda c0: pl.BlockSpec((n_ctx, HEAD_DIM), lambda b, h, ks, vi: (b, c0 + h))
    return pl.pallas_call(
        functools.partial(_na_kernel, nblk=nblk, qlen=qlen, klen=klen, variants=tuple(variants),
                          kh=min(NA_ROWS, rows)),
        grid_spec=pltpu.PrefetchScalarGridSpec(
            num_scalar_prefetch=2,
            grid=(bsz, heads),
            in_specs=[lat(qc), lat(kc), lat(vc), ctx(kc), ctx(vc),
                      pl.BlockSpec((None,) + toep.shape[1:], lambda b, h, ks, vi: (h, 0, 0, 0))],
            out_specs=pl.BlockSpec((n_lat, HEAD_DIM), lambda b, h, ks, vi: (b, h)),
            scratch_shapes=[pltpu.VMEM((len(variants), qlen, klen), F32),
                            pltpu.VMEM((n_lat, 2 * HEAD_DIM), BF16), pltpu.VMEM((n_ctx, 2 * HEAD_DIM), BF16)]),
        out_shape=jax.ShapeDtypeStruct((bsz * n_lat, heads * HEAD_DIM), BF16),
        compiler_params=_cparams(("parallel", "parallel"), 40),
    )(jnp.asarray(starts, jnp.int32), jnp.asarray(var_ids, jnp.int32), px, px, px, pc, pc, toep)


def _outproj_kernel(oa_ref, ob_ref, oc_ref, w_ref, x_ref, g_ref, o_ref, wb_ref):
    @pl.when((pl.program_id(1) == 0) & (pl.program_id(2) == 0))
    def _():
        wb_ref[...] = w_ref[...].astype(BF16)

    wa, wb = oa_ref.shape[1], ob_ref.shape[1]
    acc = jnp.dot(oa_ref[...], wb_ref[0:wa, :], preferred_element_type=F32)
    acc += jnp.dot(ob_ref[...], wb_ref[wa:wa + wb, :], preferred_element_type=F32)
    acc += jnp.dot(oc_ref[...], wb_ref[wa + wb:, :], preferred_element_type=F32)
    o_ref[...] = x_ref[...] + g_ref[...] * acc


def _out_proj(oa, ob, oc, w, x, gate, row_fn):
    w_all, layer = w
    kdim = w_all.shape[1]
    bsz, n, d = x.shape
    tm = min(1024, n)
    tn = min(512, d)
    nt = n // tm
    lhs = lambda a: pl.BlockSpec((tm, a.shape[1]), lambda j, b, i: (b * nt + i, 0))
    return pl.pallas_call(
        _outproj_kernel,
        grid=(d // tn, bsz, nt),
        in_specs=[lhs(oa), lhs(ob), lhs(oc),
                  pl.BlockSpec((None, kdim, tn), lambda j, b, i: (layer, 0, j)),
                  pl.BlockSpec((None, tm, tn), lambda j, b, i: (b, i, j)),
                  pl.BlockSpec((None, 1, tn), lambda j, b, i: (row_fn(b), 0, j))],
        out_specs=pl.BlockSpec((None, tm, tn), lambda j, b, i: (b, i, j)),
        out_shape=jax.ShapeDtypeStruct((bsz, n, d), F32),
        scratch_shapes=[pltpu.VMEM((kdim, tn), BF16)],
        compiler_params=_cparams(("arbitrary", "arbitrary", "arbitrary"), 56),
    )(oa, ob, oc, w_all, x, gate)


def _split3(x):
    hi = x.astype(BF16)
    r = x - hi.astype(F32)
    mid = r.astype(BF16)
    lo = (r - mid.astype(F32)).astype(BF16)
    return hi, mid, lo


def _router_kernel(x_ref, g_ref, sh_ref, sc_ref, wh_ref, wl_ref, h_ref, afft_ref, aff_ref):
    y = _rms(x_ref[...], g_ref[...])
    h = y * (1.0 + sc_ref[...]) + sh_ref[...]
    h_ref[...] = h
    hh = h.astype(BF16)
    hl = (h - hh.astype(F32)).astype(BF16)
    wh, wl = wh_ref[...], wl_ref[...]
    lt = _nt_dot(wh, hh) + _nt_dot(wh, hl) + _nt_dot(wl, hh)
    e = jnp.exp(lt - jnp.max(lt, axis=0, keepdims=True))
    afft_ref[...] = e / jnp.sum(e, axis=0, keepdims=True)
    ln = _nt_dot(hh, wh) + _nt_dot(hl, wh) + _nt_dot(hh, wl)
    en = jnp.exp(ln - jnp.max(ln, axis=-1, keepdims=True))
    aff_ref[...] = en / jnp.sum(en, axis=-1, keepdims=True)


def _router(x, g, shift, scale, row_fn, w_router):
    bsz, n, d = x.shape
    ne = w_router.shape[1]
    tm = min(256, n)
    wt = w_router.T
    wh = wt.astype(BF16)
    wl = (wt - wh.astype(F32)).astype(BF16)
    wspec = pl.BlockSpec((ne, d), lambda b, i: (0, 0))
    return pl.pallas_call(
        _router_kernel,
        grid=(bsz, n // tm),
        in_specs=[pl.BlockSpec((None, tm, d), lambda b, i: (b, i, 0)),
                  pl.BlockSpec((1, d), lambda b, i: (0, 0)),
                  _row_spec(d, row_fn), _row_spec(d, row_fn), wspec, wspec],
        out_specs=[pl.BlockSpec((None, tm, d), lambda b, i: (b, i, 0)),
                   pl.BlockSpec((None, ne, tm), lambda b, i: (b, 0, i)),
                   pl.BlockSpec((None, tm, ne), lambda b, i: (b, i, 0))],
        out_shape=[jax.ShapeDtypeStruct((bsz, n, d), F32),
                   jax.ShapeDtypeStruct((bsz, ne, n), F32),
                   jax.ShapeDtypeStruct((bsz, n, ne), F32)],
        compiler_params=_cparams(("parallel", "parallel"), 40),
    )(x, g.reshape(1, d), shift, scale, wh, wl)


def _lane_cumsum(x01, tri):
    r, n = x01.shape
    carry = jnp.zeros((r, 1), F32)
    out = []
    for c in range(n // HEAD_DIM):
        blk = x01[:, c * HEAD_DIM:(c + 1) * HEAD_DIM].astype(BF16)
        cs = jnp.dot(blk, tri, preferred_element_type=F32) + carry
        out.append(cs)
        carry = cs[:, HEAD_DIM - 1:HEAD_DIM]
    return jnp.concatenate(out, axis=1) if len(out) > 1 else out[0]


def _select_kernel(afft_ref, r_ref, o_ref, *, cap):
    a = afft_ref[...]
    ne, n = a.shape
    bits = pltpu.bitcast(a, jnp.int32)

    def bisect(_, lohi):
        lo, hi = lohi
        mid = lo + lax.shift_right_logical(hi - lo, 1)
        cnt = jnp.sum((bits >= mid).astype(F32), axis=1, keepdims=True)
        ok = cnt >= cap
        return jnp.where(ok, mid, lo), jnp.where(ok, hi, mid)

    lo0 = jnp.zeros((ne, 1), jnp.int32)
    hi0 = jnp.full((ne, 1), 0x7F800000, jnp.int32)
    thr, _ = lax.fori_loop(0, 32, bisect, (lo0, hi0))

    ri = lax.broadcasted_iota(jnp.int32, (HEAD_DIM, HEAD_DIM), 0)
    ci = lax.broadcasted_iota(jnp.int32, (HEAD_DIM, HEAD_DIM), 1)
    tri = (ri <= ci).astype(BF16)
    gt = (bits > thr).astype(F32)
    eq = (bits == thr).astype(F32)
    need = cap - jnp.sum(gt, axis=1, keepdims=True)
    sel = gt + eq * (_lane_cumsum(eq, tri) <= need).astype(F32)
    pos = _lane_cumsum(sel, tri) - 1.0

    slot = lax.broadcasted_iota(jnp.int32, (cap, n), 0).astype(F32)
    rmat = r_ref[...]
    for e in range(ne):
        onehot = jnp.where((pos[e:e + 1, :] == slot) & (sel[e:e + 1, :] > 0.5), 1.0, 0.0).astype(BF16)
        o_ref[e] = jnp.dot(onehot, rmat, preferred_element_type=F32)


def _select(afft, aff, cap):
    bsz, ne, n = afft.shape
    t = jnp.arange(n)
    hi, mid, lo = _split3(aff)
    tcols = jnp.stack([t // 64, t % 64], axis=1).astype(BF16)
    pad = jnp.zeros((bsz, n, HEAD_DIM - 2 - 3 * ne), BF16)
    rmat = jnp.concatenate([jnp.broadcast_to(tcols[None], (bsz, n, 2)), hi, mid, lo, pad], axis=2)
    tab = pl.pallas_call(
        functools.partial(_select_kernel, cap=cap),
        grid=(bsz,),
        in_specs=[pl.BlockSpec((None, ne, n), lambda b: (b, 0, 0)),
                  pl.BlockSpec((None, n, HEAD_DIM), lambda b: (b, 0, 0))],
        out_specs=pl.BlockSpec((None, ne, cap, HEAD_DIM), lambda b: (b, 0, 0, 0)),
        out_shape=jax.ShapeDtypeStruct((bsz, ne, cap, HEAD_DIM), F32),
        compiler_params=_cparams(("parallel",), 40),
    )(afft, rmat)
    idx = (tab[..., 0] * 64.0 + tab[..., 1]).astype(jnp.int32)
    parts = tab[..., 2:2 + 3 * ne].reshape(bsz, ne, cap, 3, ne).sum(axis=3)
    gates = (parts * jnp.eye(ne, dtype=F32)[None, :, None, :]).sum(axis=-1, keepdims=True)
    return idx, gates


def _ffn_up_kernel(*refs, caps, ne, nb):
    ns = len(caps)
    idx_refs, h_refs = refs[:ns], refs[ns:2 * ns]
    wg_ref, wu_ref, o_ref, xg_ref, sem = refs[2 * ns:]
    rows = sum(caps)
    e, b = pl.program_id(0), pl.program_id(1)
    step = e * nb + b
    slot = lax.rem(step, 2)

    def gather(e_, b_, slot_):
        r0 = 0
        for idx_ref, h_hbm, cap in zip(idx_refs, h_refs, caps):
            base = (b_ * ne + e_) * cap
            for j in range(cap):
                pltpu.make_async_copy(h_hbm.at[b_, pl.ds(idx_ref[base + j], 1), :],
                                      xg_ref.at[slot_, pl.ds(r0 + j, 1), :], sem.at[slot_]).start()
            r0 += cap

    def wait_all(slot_):
        pltpu.make_async_copy(xg_ref.at[slot_], xg_ref.at[slot_], sem.at[slot_]).wait()

    @pl.when(step == 0)
    def _():
        gather(e, b, 0)

    wait_all(slot)

    last = step + 1 >= ne * nb
    nxt = jnp.where(last, step, step + 1)
    gather(lax.div(nxt, nb), lax.rem(nxt, nb), 1 - slot)

    x = xg_ref[slot].astype(BF16)
    gate = jnp.dot(x, wg_ref[...].astype(BF16), preferred_element_type=F32)
    up = jnp.dot(x, wu_ref[...].astype(BF16), preferred_element_type=F32)
    o_ref[...] = (gate * jax.nn.sigmoid(gate) * up).astype(o_ref.dtype)

    @pl.when(last)
    def _():
        wait_all(1 - slot)


def _ffn_up(streams, wg, wu):
    caps = tuple(c for _, _, c in streams)
    bsz, _, d = streams[0][1].shape
    (wg_all, layer), (wu_all, _) = wg, wu
    ne, _, ff = wg_all.shape[1:]
    ns = len(streams)
    imap = lambda f: (lambda e, b, *idx: f(e, b))
    wspec = pl.BlockSpec((None, None, d, ff), imap(lambda e, b: (layer, e, 0, 0)))
    rows = sum(caps)
    return pl.pallas_call(
        functools.partial(_ffn_up_kernel, caps=caps, ne=ne, nb=bsz),
        grid_spec=pltpu.PrefetchScalarGridSpec(
            num_scalar_prefetch=ns,
            grid=(ne, bsz),
            in_specs=[pl.BlockSpec(memory_space=pl.ANY)] * ns + [wspec, wspec],
            out_specs=pl.BlockSpec((None, None, rows, ff), imap(lambda e, b: (b, e, 0, 0))),
            scratch_shapes=[pltpu.VMEM((2, rows, d), F32), pltpu.SemaphoreType.DMA((2,))]),
        out_shape=jax.ShapeDtypeStruct((bsz, ne, rows, ff), BF16),
        compiler_params=_cparams(("arbitrary", "arbitrary"), 58),
    )(*[i for i, _, _ in streams], *[h for _, h, _ in streams], wg_all, wu_all)


ACC_FULL_WIDTH_BYTES = 8 * MIB
SCATTER_ROWS = 4


def _ffn_down_kernel(idx_ref, hid_ref, wd_ref, gate_ref, x_ref, g2_ref, o_ref, acc_ref, ye_ref, *, cap, row0, ne, tr):
    b, e = pl.program_id(0), pl.program_id(2)

    @pl.when(e == 0)
    def _():
        acc_ref[...] = jnp.zeros_like(acc_ref)

    @pl.when(e < ne)
    def _():
        hid = hid_ref[row0:row0 + cap, :]
        ye_ref[...] = jnp.dot(hid, wd_ref[...].astype(BF16), preferred_element_type=F32) * gate_ref[...]
        base = (b * ne + e) * cap

        for j0 in range(0, cap, SCATTER_ROWS):
            toks = [idx_ref[base + j0 + u] for u in range(SCATTER_ROWS)]
            rows = [acc_ref[pl.ds(t, 1), :] for t in toks]
            for u, (t, r) in enumerate(zip(toks, rows)):
                acc_ref[pl.ds(t, 1), :] = r + ye_ref[j0 + u:j0 + u + 1, :]

    @pl.when(e >= ne)
    def _():
        r0 = pl.multiple_of((e - ne) * tr, tr)
        o_ref[...] = x_ref[...] + g2_ref[...] * acc_ref[pl.ds(r0, tr), :]


def _ffn_down(idx_flat, hid, row0, wd, gates, x, g2, row_fn):
    bsz, n, d = x.shape
    wd_all, layer = wd
    ne, ff = wd_all.shape[1:3]
    cap = gates.shape[2]
    rows = hid.shape[2]
    assert cap % SCATTER_ROWS == 0
    cb = d if n * d * 4 <= ACC_FULL_WIDTH_BYTES else min(1024, d)
    tr = min(512, n)
    ex = lambda e: jnp.minimum(e, ne - 1)
    rt = lambda e: jnp.maximum(e - ne, 0)
    return pl.pallas_call(
        functools.partial(_ffn_down_kernel, cap=cap, row0=row0, ne=ne, tr=tr),
        grid_spec=pltpu.PrefetchScalarGridSpec(
            num_scalar_prefetch=1,
            grid=(bsz, d // cb, ne + n // tr),
            in_specs=[pl.BlockSpec((None, None, rows, ff), lambda b, c, e, idx: (b, ex(e), 0, 0)),
                      pl.BlockSpec((None, None, ff, cb), lambda b, c, e, idx: (layer, ex(e), 0, c)),
                      pl.BlockSpec((None, None, cap, 1), lambda b, c, e, idx: (b, ex(e), 0, 0)),
                      pl.BlockSpec((None, tr, cb), lambda b, c, e, idx: (b, rt(e), c)),
                      pl.BlockSpec((None, 1, cb), lambda b, c, e, idx: (row_fn(b), 0, c))],
            out_specs=pl.BlockSpec((None, tr, cb), lambda b, c, e, idx: (b, rt(e), c)),
            scratch_shapes=[pltpu.VMEM((n, cb), F32), pltpu.VMEM((cap, cb), F32)]),
        out_shape=jax.ShapeDtypeStruct((bsz, n, d), F32),
        compiler_params=_cparams(("parallel", "parallel", "arbitrary"), 48),
    )(idx_flat, hid, wd_all, gates, x, g2)


def _route(x, norm_g, shift, scale, row_fn, w_router):
    n = x.shape[1]
    cap = EC_CAPACITY * n // w_router.shape[1]
    h, afft, aff = _router(x, norm_g, shift, scale, row_fn, w_router)
    idx, gates = _select(afft, aff, cap)
    return h, idx.reshape(-1), gates, cap


def _expert_ffn_residual(streams, norm_g, shift, scale, g2, w_router, wg, wu, wd):
    routed = [_route(x, norm_g, shift, scale, row_fn, w_router) for x, row_fn in streams]
    hid = _ffn_up([(idx, h, cap) for h, idx, _, cap in routed], wg, wu)
    outs, row0 = [], 0
    for (x, row_fn), (_, idx, gates, cap) in zip(streams, routed):
        outs.append(_ffn_down(idx, hid, row0, wd, gates, x, g2, row_fn))
        row0 += cap
    return outs


def kernel(x, c, ctx, c_ctx, w_ada, b_ada, norm1_g, norm2_g, w_in, w_out, a_lambda, a_subln_g, b_q_norm_g,
           b_k_norm_g, c_rpb, w_router, w_e_gate, w_e_up, w_e_down, final_g):
    bsz, n, d = x.shape
    n_ctx = ctx.shape[1]
    depth = w_ada.shape[0]
    rows = n // GRID_W
    unit_heads = d // 8 // HEAD_DIM
    a_heads, b_heads, c_heads = 2 * unit_heads, 3 * unit_heads, 3 * unit_heads
    assert bsz + 1 <= MOD_ROWS and rows >= 1 and n % GRID_W == 0

    cvec = jnp.zeros((MOD_ROWS, d), F32).at[:bsz].set(c).at[bsz].set(c_ctx)
    mod = _ada_mod(cvec, w_ada, b_ada)
    lat_row = lambda b: b
    ctx_row = lambda b: bsz

    tabs_a, tabs_b = _rope_tables(n, "A"), _rope_tables(n, "B")
    na_toep = _na_toeplitz(c_rpb)

    for l in range(depth):
        need_ctx = l < depth - 1
        lam_init = 0.8 - 0.6 * float(np.exp(-0.3 * l))
        m6 = [mod[l, :, k * d:(k + 1) * d].reshape(MOD_ROWS, 1, d) for k in range(N_MOD)]
        sh1, sc1, g1, sh2, sc2, g2 = m6
        w_in_l, w_out_l = (w_in, l), (w_out, l)
        wg, wu, wd = (w_e_gate, l), (w_e_up, l), (w_e_down, l)

        hx = _norm_mod(x, norm1_g[l], sh1, sc1, lat_row, BF16).reshape(bsz * n, d)
        hc = _norm_mod(ctx, norm1_g[l], sh1, sc1, ctx_row, BF16).reshape(bsz * n_ctx, d)
        px, pc = _in_proj(hx, hc, w_in_l, tabs_a, tabs_b, b_q_norm_g[l], b_k_norm_g[l], n)

        common = dict(n_ctx=n_ctx, n_lat=n, unit_heads=unit_heads)
        a_args = dict(kv_heads=a_heads, group=1, q_unit=UNIT_AQ, k_unit=UNIT_AK, v_unit=UNIT_AV, diff=True,
                      lam=a_lambda[l], subln_g=a_subln_g[l], lam_init=lam_init)
        b_args = dict(kv_heads=unit_heads, group=b_heads // unit_heads, q_unit=UNIT_BQ, k_unit=UNIT_BK, v_unit=UNIT_BV)
        oa = _attention(px, pc, px, n_q=n, **a_args, **common)
        ob = _attention(px, pc, px, n_q=n, **b_args, **common)
        oc = _neighbourhood(px, pc, na_toep[l], n_lat=n, n_ctx=n_ctx, heads=c_heads, unit_heads=unit_heads)
        x = _out_proj(oa, ob, oc, w_out_l, x, g1, lat_row)
        streams = [(x, lat_row)]
        if need_ctx:
            c_args = dict(kv_heads=c_heads, group=1, q_unit=UNIT_CQ, k_unit=UNIT_CK, v_unit=UNIT_CV)
            oa_c = _attention(pc, pc, None, n_q=n_ctx, **a_args, **common)
            ob_c = _attention(pc, pc, None, n_q=n_ctx, **b_args, **common)
            oc_c = _attention(pc, pc, None, n_q=n_ctx, **c_args, **common)
            ctx = _out_proj(oa_c, ob_c, oc_c, w_out_l, ctx, g1, ctx_row)
            streams.append((ctx, ctx_row))
        outs = _expert_ffn_residual(streams, norm2_g[l], sh2, sc2, g2, w_router[l], wg, wu, wd)
        x = outs[0]
        if need_ctx:
            ctx = outs[1]

    return _final_norm(x, final_g)
```

```python
import functools

import numpy as np
import jax
import jax.numpy as jnp
from jax import lax
from jax.experimental import pallas as pl
from jax.experimental.pallas import tpu as pltpu

HEAD_DIM = 128
A_QK_DIM = HEAD_DIM // 2
GRID_W = 64
NA_ROWS = 8
NA_COLS = 16
EC_CAPACITY = 2
N_MOD = 6
ROPE_BASE = 10000.0
EPS = 1e-6
NEG = -1e30
LOG2E = 1.4426950408889634
A_Q_SCALE = A_QK_DIM ** -0.5 * LOG2E
Q_SCALE = HEAD_DIM ** -0.5 * LOG2E
MIB = 1024 * 1024
MOD_ROWS = 8
F32 = jnp.float32
BF16 = jnp.bfloat16

UNIT_AQ, UNIT_AK, UNIT_AV = 0, 2, 4
UNIT_BQ, UNIT_BK, UNIT_BV = 6, 9, 10
UNIT_CQ, UNIT_CK, UNIT_CV = 11, 14, 17
N_UNITS = 20


def _cparams(sem, vmem_mib):
    return pltpu.CompilerParams(dimension_semantics=sem, vmem_limit_bytes=vmem_mib * MIB)


def _nt_dot(a, b):
    return lax.dot_general(a, b, (((1,), (1,)), ((), ())), preferred_element_type=F32)


def _ada_kernel(c_ref, w_ref, b_ref, o_ref):
    c = c_ref[...]
    s = (c * jax.nn.sigmoid(c)).astype(BF16)
    o_ref[...] = jnp.dot(s, w_ref[...].astype(BF16), preferred_element_type=F32) + b_ref[...]


def _ada_mod(cvec, w_ada, b_ada):
    depth, d, nm = w_ada.shape
    tn = min(512, nm)
    return pl.pallas_call(
        _ada_kernel,
        grid=(depth, nm // tn),
        in_specs=[pl.BlockSpec((MOD_ROWS, d), lambda l, j: (0, 0)),
                  pl.BlockSpec((None, d, tn), lambda l, j: (l, 0, j)),
                  pl.BlockSpec((None, 1, tn), lambda l, j: (l, 0, j))],
        out_specs=pl.BlockSpec((None, MOD_ROWS, tn), lambda l, j: (l, 0, j)),
        out_shape=jax.ShapeDtypeStruct((depth, MOD_ROWS, nm), F32),
        compiler_params=_cparams(("parallel", "parallel"), 40),
    )(cvec, w_ada, b_ada.reshape(depth, 1, nm))


def _rms(x, g):
    return x * lax.rsqrt(jnp.mean(x * x, axis=-1, keepdims=True) + EPS) * g


def _norm_mod_kernel(x_ref, g_ref, sh_ref, sc_ref, o_ref):
    y = _rms(x_ref[...], g_ref[...])
    o_ref[...] = (y * (1.0 + sc_ref[...]) + sh_ref[...]).astype(o_ref.dtype)


def _norm_kernel(x_ref, g_ref, o_ref):
    o_ref[...] = _rms(x_ref[...], g_ref[...]).astype(o_ref.dtype)


def _row_spec(d, row_fn):
    return pl.BlockSpec((None, 1, d), lambda b, i: (row_fn(b), 0, 0))


def _norm_mod(x, g, shift, scale, row_fn, out_dtype):
    bsz, n, d = x.shape
    tm = min(256, n)
    return pl.pallas_call(
        _norm_mod_kernel,
        grid=(bsz, n // tm),
        in_specs=[pl.BlockSpec((None, tm, d), lambda b, i: (b, i, 0)),
                  pl.BlockSpec((1, d), lambda b, i: (0, 0)),
                  _row_spec(d, row_fn), _row_spec(d, row_fn)],
        out_specs=pl.BlockSpec((None, tm, d), lambda b, i: (b, i, 0)),
        out_shape=jax.ShapeDtypeStruct((bsz, n, d), out_dtype),
        compiler_params=_cparams(("parallel", "parallel"), 32),
    )(x, g.reshape(1, d), shift, scale)


def _final_norm(x, g):
    bsz, n, d = x.shape
    tm = min(256, n)
    return pl.pallas_call(
        _norm_kernel,
        grid=(bsz, n // tm),
        in_specs=[pl.BlockSpec((None, tm, d), lambda b, i: (b, i, 0)),
                  pl.BlockSpec((1, d), lambda b, i: (0, 0))],
        out_specs=pl.BlockSpec((None, tm, d), lambda b, i: (b, i, 0)),
        out_shape=jax.ShapeDtypeStruct((bsz, n, d), x.dtype),
        compiler_params=_cparams(("parallel", "parallel"), 32),
    )(x, g.reshape(1, d))


def _rope_tables(n, group):
    p = jnp.arange(HEAD_DIM)
    half = 16 if group == "A" else 32
    use_col = ((p // (2 * half)) % 2) == 1
    first = (p % (2 * half)) < half
    freqs = ROPE_BASE ** (-(p % half).astype(F32) / half)
    t = jnp.arange(n)
    pos = jnp.where(use_col[None, :], (t % GRID_W)[:, None], (t // GRID_W)[:, None]).astype(F32)
    ang = pos * freqs[None, :]
    return jnp.cos(ang), jnp.where(first[None, :], -jnp.sin(ang), jnp.sin(ang))


def _rotate(x, cos, sin_signed, half):
    lane = lax.broadcasted_iota(jnp.int32, x.shape, 1)
    first = (lane % (2 * half)) < half
    partner = jnp.where(first, pltpu.roll(x, HEAD_DIM - half, 1), pltpu.roll(x, half, 1))
    return x * cos + partner * sin_signed


INPROJ_SUBTILES = 2


def _inproj_emit(j, h_ref, wb_ref, rope_a, rope_b, gq_ref, gk_ref, o_ref):
    tm = h_ref.shape[0]
    sub = tm // INPROJ_SUBTILES if tm % (16 * INPROJ_SUBTILES) == 0 else tm
    heads = [slice(i * HEAD_DIM, (i + 1) * HEAD_DIM) for i in range(wb_ref.shape[1] // HEAD_DIM)]

    def tiles():
        for r in range(0, tm, sub):
            rows = slice(r, r + sub)
            yield rows, jnp.dot(h_ref[rows, :], wb_ref[...], preferred_element_type=F32)

    def rot(x, tabs, half, rows):
        return x if tabs is None else _rotate(x, tabs[0][rows, :], tabs[1][rows, :], half)

    @pl.when(j < UNIT_AV)
    def _():
        scale = jnp.where(j < UNIT_AK, A_Q_SCALE, 1.0).astype(F32)
        for rows, acc in tiles():
            for s in heads:
                o_ref[rows, s] = (rot(acc[:, s], rope_a, 16, rows) * scale).astype(o_ref.dtype)

    @pl.when((j >= UNIT_BQ) & (j < UNIT_BV))
    def _():
        is_q = j < UNIT_BK
        g = jnp.where(is_q, gq_ref[...], gk_ref[...])
        scale = jnp.where(is_q, Q_SCALE, 1.0).astype(F32)
        for rows, acc in tiles():
            for s in heads:
                o_ref[rows, s] = (rot(_rms(acc[:, s], g), rope_b, 32, rows) * scale).astype(o_ref.dtype)

    @pl.when((j >= UNIT_CQ) & (j < UNIT_CK))
    def _():
        for rows, acc in tiles():
            o_ref[rows, :] = (acc * Q_SCALE).astype(o_ref.dtype)

    @pl.when(((j >= UNIT_AV) & (j < UNIT_BQ)) | (j == UNIT_BV) | (j >= UNIT_CK))
    def _():
        for rows, acc in tiles():
            o_ref[rows, :] = acc.astype(o_ref.dtype)


def _inproj_kernel(hx_ref, hc_ref, w_ref, ca_ref, sa_ref, cb_ref, sb_ref, gq_ref, gk_ref, ox_ref, oc_ref, wb_ref, *, nt):
    j, i = pl.program_id(0), pl.program_id(1)

    @pl.when(i == 0)
    def _():
        wb_ref[...] = w_ref[...].astype(BF16)

    @pl.when(i < nt)
    def _():
        _inproj_emit(j, hx_ref, wb_ref, (ca_ref, sa_ref), (cb_ref, sb_ref), gq_ref, gk_ref, ox_ref)

    @pl.when(i == nt)
    def _():
        _inproj_emit(j, hc_ref, wb_ref, None, None, gq_ref, gk_ref, oc_ref)


def _in_proj(hx, hc, w, tabs_a, tabs_b, gq, gk, n_per_sample):
    w_all, layer = w
    m, d = hx.shape
    mc = hc.shape[0]
    wid = w_all.shape[2]
    tn = wid // N_UNITS
    tm = min(1024, n_per_sample)
    tiles = n_per_sample // tm
    nt = m // tm
    lat = lambda i: jnp.minimum(i, nt - 1)
    tab = pl.BlockSpec((tm, HEAD_DIM), lambda j, i: (lat(i) % tiles, 0))
    vec = pl.BlockSpec((1, HEAD_DIM), lambda j, i: (0, 0))
    return pl.pallas_call(
        functools.partial(_inproj_kernel, nt=nt),
        grid=(N_UNITS, nt + 1),
        in_specs=[pl.BlockSpec((tm, d), lambda j, i: (lat(i), 0)),
                  pl.BlockSpec((mc, d), lambda j, i: (0, 0)),
                  pl.BlockSpec((None, d, tn), lambda j, i: (layer, 0, j)),
                  tab, tab, tab, tab, vec, vec],
        out_specs=[pl.BlockSpec((tm, tn), lambda j, i: (lat(i), j)),
                   pl.BlockSpec((mc, tn), lambda j, i: (0, j))],
        out_shape=[jax.ShapeDtypeStruct((m, wid), BF16), jax.ShapeDtypeStruct((mc, wid), BF16)],
        scratch_shapes=[pltpu.VMEM((d, tn), BF16)],
        compiler_params=_cparams(("arbitrary", "arbitrary"), 56),
    )(hx, hc, w_all, tabs_a[0], tabs_a[1], tabs_b[0], tabs_b[1], gq.reshape(1, HEAD_DIM), gk.reshape(1, HEAD_DIM))


ATTN_TILE_ROWS = 1024
ATTN_SUB_ROWS = 256


def _attn_kernel(*refs, diff, has_lat, group, lam_init):
    it = iter(refs)
    q_ref, kc_ref, vc_ref = next(it), next(it), next(it)
    kl_ref = vl_ref = lam_ref = g_ref = None
    if has_lat:
        kl_ref, vl_ref = next(it), next(it)
    if diff:
        lam_ref, g_ref = next(it), next(it)
    o_ref, k_scr, v_scr = next(it), next(it), next(it)
    n_ctx = kc_ref.shape[0]

    @pl.when(pl.program_id(2) == 0)
    def _():
        k_scr[0:n_ctx, :] = kc_ref[...]
        v_scr[0:n_ctx, 0:HEAD_DIM] = vc_ref[...]
        if has_lat:
            k_scr[n_ctx:, :] = kl_ref[...]
            v_scr[n_ctx:, 0:HEAD_DIM] = vl_ref[...]
        v_scr[:, HEAD_DIM:] = jnp.ones((v_scr.shape[0], HEAD_DIM), BF16)

    k, v = k_scr[...], v_scr[...]

    def softmax_pv(q):
        s = _nt_dot(q, k)
        p = jnp.exp2(s - jnp.max(s, axis=-1, keepdims=True)).astype(BF16)
        oe = jnp.dot(p, v, preferred_element_type=F32)
        return oe[:, :HEAD_DIM] * (1.0 / oe[:, HEAD_DIM:])

    sub = min(ATTN_SUB_ROWS, q_ref.shape[0])
    for r0 in range(0, q_ref.shape[0], sub):
        rows = slice(r0, r0 + sub)
        for g in range(group):
            cols = slice(g * HEAD_DIM, (g + 1) * HEAD_DIM)
            q = q_ref[rows, cols]
            if diff:
                qf = q.astype(F32)
                lane = lax.broadcasted_iota(jnp.int32, qf.shape, 1)
                o1 = softmax_pv(jnp.where(lane < A_QK_DIM, qf, 0.0).astype(BF16))
                o2 = softmax_pv(jnp.where(lane >= A_QK_DIM, qf, 0.0).astype(BF16))
                lv = lam_ref[...]
                lam = (jnp.exp(jnp.sum(lv[0:1] * lv[1:2], axis=-1, keepdims=True))
                       - jnp.exp(jnp.sum(lv[2:3] * lv[3:4], axis=-1, keepdims=True)) + lam_init)
                o = _rms(o1 - lam * o2, g_ref[...]) * (1.0 - lam_init)
            else:
                o = softmax_pv(q)
            o_ref[rows, cols] = o.astype(o_ref.dtype)


def _attention(pq, pc, px, *, n_q, n_ctx, n_lat, kv_heads, group, q_unit, k_unit, v_unit, unit_heads,
               diff=False, lam=None, subln_g=None, lam_init=0.0):
    bsz = pq.shape[0] // n_q
    has_lat = px is not None
    tq = min(ATTN_TILE_ROWS, n_q)
    nqt = n_q // tq
    gw = group * HEAD_DIM
    n_keys = n_ctx + (n_lat if has_lat else 0)
    qc, kc, vc = q_unit * unit_heads // group, k_unit * unit_heads, v_unit * unit_heads
    assert (q_unit * unit_heads) % group == 0
    in_specs = [pl.BlockSpec((tq, gw), lambda b, h, i: (b * nqt + i, qc + h)),
                pl.BlockSpec((n_ctx, HEAD_DIM), lambda b, h, i: (b, kc + h)),
                pl.BlockSpec((n_ctx, HEAD_DIM), lambda b, h, i: (b, vc + h))]
    args = [pq, pc, pc]
    if has_lat:
        in_specs += [pl.BlockSpec((n_lat, HEAD_DIM), lambda b, h, i: (b, kc + h)),
                     pl.BlockSpec((n_lat, HEAD_DIM), lambda b, h, i: (b, vc + h))]
        args += [px, px]
    if diff:
        in_specs += [pl.BlockSpec((4, A_QK_DIM), lambda b, h, i: (0, 0)),
                     pl.BlockSpec((1, HEAD_DIM), lambda b, h, i: (0, 0))]
        args += [lam, subln_g.reshape(1, HEAD_DIM)]
    return pl.pallas_call(
        functools.partial(_attn_kernel, diff=diff, has_lat=has_lat, group=group, lam_init=lam_init),
        grid=(bsz, kv_heads, nqt),
        in_specs=in_specs,
        out_specs=pl.BlockSpec((tq, gw), lambda b, h, i: (b * nqt + i, h)),
        out_shape=jax.ShapeDtypeStruct((bsz * n_q, kv_heads * gw), BF16),
        scratch_shapes=[pltpu.VMEM((n_keys, HEAD_DIM), BF16), pltpu.VMEM((n_keys, 2 * HEAD_DIM), BF16)],
        compiler_params=_cparams(("parallel", "parallel", "arbitrary"), 48),
    )(*args)


NA_QROWS = 4


def _na_plan(rows):
    kh = min(NA_ROWS, rows)
    qb = min(NA_QROWS, rows)
    assert rows % qb == 0
    kw = min(rows, kh + qb)
    starts, var_ids, variants = [], [], []
    for r0 in range(0, rows, qb):
        rs = [int(np.clip(r - kh // 2, 0, rows - kh)) for r in range(r0, r0 + qb)]
        kstart = min(min(rs), rows - kw)
        assert kstart >= 0 and max(rs) + kh <= kstart + kw
        sig = tuple((r0 + u - kstart, rs[u] - kstart) for u in range(qb))
        if sig not in variants:
            variants.append(sig)
        starts.append(kstart)
        var_ids.append(variants.index(sig))
    return kw, starts, var_ids, variants


def _na_toeplitz(c_rpb):
    q = np.arange(GRID_W)
    kc = np.arange(GRID_W)
    start = np.clip(q - NA_COLS // 2, 0, GRID_W - NA_COLS)
    inwin = (kc[None, :] >= start[:, None]) & (kc[None, :] < start[:, None] + NA_COLS)
    ci = kc[None, :] - q[:, None] + NA_COLS - 1
    onehot = ((ci[..., None] == np.arange(2 * NA_COLS - 1)) & inwin[..., None]).astype(np.float32)
    toep = jnp.einsum("lhab,qcb->lhaqc", c_rpb.astype(F32), jnp.asarray(onehot), precision=lax.Precision.HIGHEST)
    return jnp.where(jnp.asarray(inwin)[None, None, None], toep * LOG2E, NEG)


def _na_kernel(kst_ref, var_ref, q_ref, k_ref, v_ref, kc_ref, vc_ref, toep_ref, o_ref, t_ref, va_ref, vca_ref, *,
               nblk, qlen, klen, variants, kh):
    kctx = kc_ref[...]
    va_ref[:, 0:HEAD_DIM] = v_ref[...]
    va_ref[:, HEAD_DIM:] = jnp.ones((va_ref.shape[0], HEAD_DIM), BF16)
    vca_ref[:, 0:HEAD_DIM] = vc_ref[...]
    vca_ref[:, HEAD_DIM:] = jnp.ones((vca_ref.shape[0], HEAD_DIM), BF16)
    vctx = vca_ref[...]

    masked = jnp.full((GRID_W, GRID_W), NEG, F32)
    for vi, sig in enumerate(variants):
        for u, (qoff, woff) in enumerate(sig):
            tile = lambda kr: toep_ref[kr - qoff + NA_ROWS - 1] if woff <= kr < woff + kh else masked
            for kr in range(0, klen // GRID_W, 2):
                t_ref[vi, u * GRID_W:(u + 1) * GRID_W, kr * GRID_W:(kr + 2) * GRID_W] = jnp.concatenate(
                    [tile(kr), tile(kr + 1)], axis=1)

    def body(rb, carry):
        qoff = pl.multiple_of(rb * qlen, qlen)
        koff = pl.multiple_of(kst_ref[rb] * GRID_W, GRID_W)
        q = q_ref[pl.ds(qoff, qlen), :]
        s_loc = _nt_dot(q, k_ref[pl.ds(koff, klen), :]) + t_ref[var_ref[rb]]
        s_ctx = _nt_dot(q, kctx)
        m = jnp.maximum(jnp.max(s_loc, axis=-1, keepdims=True), jnp.max(s_ctx, axis=-1, keepdims=True))
        p_loc = jnp.exp2(s_loc - m).astype(BF16)
        p_ctx = jnp.exp2(s_ctx - m).astype(BF16)
        oe = (jnp.dot(p_loc, va_ref[pl.ds(koff, klen), :], preferred_element_type=F32)
              + jnp.dot(p_ctx, vctx, preferred_element_type=F32))
        o_ref[pl.ds(qoff, qlen), :] = (oe[:, :HEAD_DIM] * (1.0 / oe[:, HEAD_DIM:])).astype(o_ref.dtype)
        return carry

    lax.fori_loop(0, nblk, body, 0, unroll=4 if nblk % 4 == 0 else 1)


def _neighbourhood(px, pc, toep, *, n_lat, n_ctx, heads, unit_heads):
    bsz = px.shape[0] // n_lat
    rows = n_lat // GRID_W
    kw, starts, var_ids, variants = _na_plan(rows)
    assert kw % 2 == 0
    nblk = len(starts)
    qlen, klen = n_lat // nblk, kw * GRID_W
    qc, kc, vc = UNIT_CQ * unit_heads, UNIT_CK * unit_heads, UNIT_CV * unit_heads
    lat = lambda c0: pl.BlockSpec((n_lat, HEAD_DIM), lambda b, h, ks, vi: (b, c0 + h))
    ctx = lambda c0: pl.BlockSpec((n_ctx, HEAD_DIM), lambda b, h, ks, vi: (b, c0 + h))
    return pl.pallas_call(
        functools.partial(_na_kernel, nblk=nblk, qlen=qlen, klen=klen, variants=tuple(variants),
                          kh=min(NA_ROWS, rows)),
        grid_spec=pltpu.PrefetchScalarGridSpec(
            num_scalar_prefetch=2,
            grid=(bsz, heads),
            in_specs=[lat(qc), lat(kc), lat(vc), ctx(kc), ctx(vc),
                      pl.BlockSpec((None,) + toep.shape[1:], lambda b, h, ks, vi: (h, 0, 0, 0))],
            out_specs=pl.BlockSpec((n_lat, HEAD_DIM), lambda b, h, ks, vi: (b, h)),
            scratch_shapes=[pltpu.VMEM((len(variants), qlen, klen), F32),
                            pltpu.VMEM((n_lat, 2 * HEAD_DIM), BF16), pltpu.VMEM((n_ctx, 2 * HEAD_DIM), BF16)]),
        out_shape=jax.ShapeDtypeStruct((bsz * n_lat, heads * HEAD_DIM), BF16),
        compiler_params=_cparams(("parallel", "parallel"), 40),
    )(jnp.asarray(starts, jnp.int32), jnp.asarray(var_ids, jnp.int32), px, px, px, pc, pc, toep)


def _outproj_kernel(oa_ref, ob_ref, oc_ref, w_ref, x_ref, g_ref, o_ref, wb_ref):
    @pl.when((pl.program_id(1) == 0) & (pl.program_id(2) == 0))
    def _():
        wb_ref[...] = w_ref[...].astype(BF16)

    wa, wb = oa_ref.shape[1], ob_ref.shape[1]
    acc = jnp.dot(oa_ref[...], wb_ref[0:wa, :], preferred_element_type=F32)
    acc += jnp.dot(ob_ref[...], wb_ref[wa:wa + wb, :], preferred_element_type=F32)
    acc += jnp.dot(oc_ref[...], wb_ref[wa + wb:, :], preferred_element_type=F32)
    o_ref[...] = x_ref[...] + g_ref[...] * acc


def _out_proj(oa, ob, oc, w, x, gate, row_fn):
    w_all, layer = w
    kdim = w_all.shape[1]
    bsz, n, d = x.shape
    tm = min(1024, n)
    tn = min(512, d)
    nt = n // tm
    lhs = lambda a: pl.BlockSpec((tm, a.shape[1]), lambda j, b, i: (b * nt + i, 0))
    return pl.pallas_call(
        _outproj_kernel,
        grid=(d // tn, bsz, nt),
        in_specs=[lhs(oa), lhs(ob), lhs(oc),
                  pl.BlockSpec((None, kdim, tn), lambda j, b, i: (layer, 0, j)),
                  pl.BlockSpec((None, tm, tn), lambda j, b, i: (b, i, j)),
                  pl.BlockSpec((None, 1, tn), lambda j, b, i: (row_fn(b), 0, j))],
        out_specs=pl.BlockSpec((None, tm, tn), lambda j, b, i: (b, i, j)),
        out_shape=jax.ShapeDtypeStruct((bsz, n, d), F32),
        scratch_shapes=[pltpu.VMEM((kdim, tn), BF16)],
        compiler_params=_cparams(("arbitrary", "arbitrary", "arbitrary"), 56),
    )(oa, ob, oc, w_all, x, gate)


def _split3(x):
    hi = x.astype(BF16)
    r = x - hi.astype(F32)
    mid = r.astype(BF16)
    lo = (r - mid.astype(F32)).astype(BF16)
    return hi, mid, lo


def _router_kernel(x_ref, g_ref, sh_ref, sc_ref, wh_ref, wl_ref, h_ref, afft_ref, aff_ref):
    y = _rms(x_ref[...], g_ref[...])
    h = y * (1.0 + sc_ref[...]) + sh_ref[...]
    h_ref[...] = h
    hh = h.astype(BF16)
    hl = (h - hh.astype(F32)).astype(BF16)
    wh, wl = wh_ref[...], wl_ref[...]
    lt = _nt_dot(wh, hh) + _nt_dot(wh, hl) + _nt_dot(wl, hh)
    e = jnp.exp(lt - jnp.max(lt, axis=0, keepdims=True))
    afft_ref[...] = e / jnp.sum(e, axis=0, keepdims=True)
    ln = _nt_dot(hh, wh) + _nt_dot(hl, wh) + _nt_dot(hh, wl)
    en = jnp.exp(ln - jnp.max(ln, axis=-1, keepdims=True))
    aff_ref[...] = en / jnp.sum(en, axis=-1, keepdims=True)


def _router(x, g, shift, scale, row_fn, w_router):
    bsz, n, d = x.shape
    ne = w_router.shape[1]
    tm = min(256, n)
    wt = w_router.T
    wh = wt.astype(BF16)
    wl = (wt - wh.astype(F32)).astype(BF16)
    wspec = pl.BlockSpec((ne, d), lambda b, i: (0, 0))
    return pl.pallas_call(
        _router_kernel,
        grid=(bsz, n // tm),
        in_specs=[pl.BlockSpec((None, tm, d), lambda b, i: (b, i, 0)),
                  pl.BlockSpec((1, d), lambda b, i: (0, 0)),
                  _row_spec(d, row_fn), _row_spec(d, row_fn), wspec, wspec],
        out_specs=[pl.BlockSpec((None, tm, d), lambda b, i: (b, i, 0)),
                   pl.BlockSpec((None, ne, tm), lambda b, i: (b, 0, i)),
                   pl.BlockSpec((None, tm, ne), lambda b, i: (b, i, 0))],
        out_shape=[jax.ShapeDtypeStruct((bsz, n, d), F32),
                   jax.ShapeDtypeStruct((bsz, ne, n), F32),
                   jax.ShapeDtypeStruct((bsz, n, ne), F32)],
        compiler_params=_cparams(("parallel", "parallel"), 40),
    )(x, g.reshape(1, d), shift, scale, wh, wl)


def _lane_cumsum(x01, tri):
    r, n = x01.shape
    carry = jnp.zeros((r, 1), F32)
    out = []
    for c in range(n // HEAD_DIM):
        blk = x01[:, c * HEAD_DIM:(c + 1) * HEAD_DIM].astype(BF16)
        cs = jnp.dot(blk, tri, preferred_element_type=F32) + carry
        out.append(cs)
        carry = cs[:, HEAD_DIM - 1:HEAD_DIM]
    return jnp.concatenate(out, axis=1) if len(out) > 1 else out[0]


def _select_kernel(afft_ref, r_ref, o_ref, *, cap):
    a = afft_ref[...]
    ne, n = a.shape
    bits = pltpu.bitcast(a, jnp.int32)

    def bisect(_, lohi):
        lo, hi = lohi
        mid = lo + lax.shift_right_logical(hi - lo, 1)
        cnt = jnp.sum((bits >= mid).astype(F32), axis=1, keepdims=True)
        ok = cnt >= cap
        return jnp.where(ok, mid, lo), jnp.where(ok, hi, mid)

    lo0 = jnp.zeros((ne, 1), jnp.int32)
    hi0 = jnp.full((ne, 1), 0x7F800000, jnp.int32)
    thr, _ = lax.fori_loop(0, 32, bisect, (lo0, hi0))

    ri = lax.broadcasted_iota(jnp.int32, (HEAD_DIM, HEAD_DIM), 0)
    ci = lax.broadcasted_iota(jnp.int32, (HEAD_DIM, HEAD_DIM), 1)
    tri = (ri <= ci).astype(BF16)
    gt = (bits > thr).astype(F32)
    eq = (bits == thr).astype(F32)
    need = cap - jnp.sum(gt, axis=1, keepdims=True)
    sel = gt + eq * (_lane_cumsum(eq, tri) <= need).astype(F32)
    pos = _lane_cumsum(sel, tri) - 1.0

    slot = lax.broadcasted_iota(jnp.int32, (cap, n), 0).astype(F32)
    rmat = r_ref[...]
    for e in range(ne):
        onehot = jnp.where((pos[e:e + 1, :] == slot) & (sel[e:e + 1, :] > 0.5), 1.0, 0.0).astype(BF16)
        o_ref[e] = jnp.dot(onehot, rmat, preferred_element_type=F32)


def _select(afft, aff, cap):
    bsz, ne, n = afft.shape
    t = jnp.arange(n)
    hi, mid, lo = _split3(aff)
    tcols = jnp.stack([t // 64, t % 64], axis=1).astype(BF16)
    pad = jnp.zeros((bsz, n, HEAD_DIM - 2 - 3 * ne), BF16)
    rmat = jnp.concatenate([jnp.broadcast_to(tcols[None], (bsz, n, 2)), hi, mid, lo, pad], axis=2)
    tab = pl.pallas_call(
        functools.partial(_select_kernel, cap=cap),
        grid=(bsz,),
        in_specs=[pl.BlockSpec((None, ne, n), lambda b: (b, 0, 0)),
                  pl.BlockSpec((None, n, HEAD_DIM), lambda b: (b, 0, 0))],
        out_specs=pl.BlockSpec((None, ne, cap, HEAD_DIM), lambda b: (b, 0, 0, 0)),
        out_shape=jax.ShapeDtypeStruct((bsz, ne, cap, HEAD_DIM), F32),
        compiler_params=_cparams(("parallel",), 40),
    )(afft, rmat)
    idx = (tab[..., 0] * 64.0 + tab[..., 1]).astype(jnp.int32)
    parts = tab[..., 2:2 + 3 * ne].reshape(bsz, ne, cap, 3, ne).sum(axis=3)
    gates = (parts * jnp.eye(ne, dtype=F32)[None, :, None, :]).sum(axis=-1, keepdims=True)
    return idx, gates


def _ffn_up_kernel(*refs, caps, ne, nb):
    ns = len(caps)
    idx_refs, h_refs = refs[:ns], refs[ns:2 * ns]
    wg_ref, wu_ref, o_ref, xg_ref, sem = refs[2 * ns:]
    rows = sum(caps)
    e, b = pl.program_id(0), pl.program_id(1)
    step = e * nb + b
    slot = lax.rem(step, 2)

    def gather(e_, b_, slot_):
        r0 = 0
        for idx_ref, h_hbm, cap in zip(idx_refs, h_refs, caps):
            base = (b_ * ne + e_) * cap
            for j in range(cap):
                pltpu.make_async_copy(h_hbm.at[b_, pl.ds(idx_ref[base + j], 1), :],
                                      xg_ref.at[slot_, pl.ds(r0 + j, 1), :], sem.at[slot_]).start()
            r0 += cap

    def wait_all(slot_):
        pltpu.make_async_copy(xg_ref.at[slot_], xg_ref.at[slot_], sem.at[slot_]).wait()

    @pl.when(step == 0)
    def _():
        gather(e, b, 0)

    wait_all(slot)

    last = step + 1 >= ne * nb
    nxt = jnp.where(last, step, step + 1)
    gather(lax.div(nxt, nb), lax.rem(nxt, nb), 1 - slot)

    x = xg_ref[slot].astype(BF16)
    gate = jnp.dot(x, wg_ref[...].astype(BF16), preferred_element_type=F32)
    up = jnp.dot(x, wu_ref[...].astype(BF16), preferred_element_type=F32)
    o_ref[...] = (gate * jax.nn.sigmoid(gate) * up).astype(o_ref.dtype)

    @pl.when(last)
    def _():
        wait_all(1 - slot)


def _ffn_up(streams, wg, wu):
    caps = tuple(c for _, _, c in streams)
    bsz, _, d = streams[0][1].shape
    (wg_all, layer), (wu_all, _) = wg, wu
    ne, _, ff = wg_all.shape[1:]
    ns = len(streams)
    imap = lambda f: (lambda e, b, *idx: f(e, b))
    wspec = pl.BlockSpec((None, None, d, ff), imap(lambda e, b: (layer, e, 0, 0)))
    rows = sum(caps)
    return pl.pallas_call(
        functools.partial(_ffn_up_kernel, caps=caps, ne=ne, nb=bsz),
        grid_spec=pltpu.PrefetchScalarGridSpec(
            num_scalar_prefetch=ns,
            grid=(ne, bsz),
            in_specs=[pl.BlockSpec(memory_space=pl.ANY)] * ns + [wspec, wspec],
            out_specs=pl.BlockSpec((None, None, rows, ff), imap(lambda e, b: (b, e, 0, 0))),
            scratch_shapes=[pltpu.VMEM((2, rows, d), F32), pltpu.SemaphoreType.DMA((2,))]),
        out_shape=jax.ShapeDtypeStruct((bsz, ne, rows, ff), BF16),
        compiler_params=_cparams(("arbitrary", "arbitrary"), 58),
    )(*[i for i, _, _ in streams], *[h for _, h, _ in streams], wg_all, wu_all)


ACC_FULL_WIDTH_BYTES = 8 * MIB
SCATTER_ROWS = 4


def _ffn_down_kernel(idx_ref, hid_ref, wd_ref, gate_ref, x_ref, g2_ref, o_ref, acc_ref, ye_ref, *, cap, row0, ne, tr):
    b, e = pl.program_id(0), pl.program_id(2)

    @pl.when(e == 0)
    def _():
        acc_ref[...] = jnp.zeros_like(acc_ref)

    @pl.when(e < ne)
    def _():
        hid = hid_ref[row0:row0 + cap, :]
        ye_ref[...] = jnp.dot(hid, wd_ref[...].astype(BF16), preferred_element_type=F32) * gate_ref[...]
        base = (b * ne + e) * cap

        for j0 in range(0, cap, SCATTER_ROWS):
            toks = [idx_ref[base + j0 + u] for u in range(SCATTER_ROWS)]
            rows = [acc_ref[pl.ds(t, 1), :] for t in toks]
            for u, (t, r) in enumerate(zip(toks, rows)):
                acc_ref[pl.ds(t, 1), :] = r + ye_ref[j0 + u:j0 + u + 1, :]

    @pl.when(e >= ne)
    def _():
        r0 = pl.multiple_of((e - ne) * tr, tr)
        o_ref[...] = x_ref[...] + g2_ref[...] * acc_ref[pl.ds(r0, tr), :]


def _ffn_down(idx_flat, hid, row0, wd, gates, x, g2, row_fn):
    bsz, n, d = x.shape
    wd_all, layer = wd
    ne, ff = wd_all.shape[1:3]
    cap = gates.shape[2]
    rows = hid.shape[2]
    assert cap % SCATTER_ROWS == 0
    cb = d if n * d * 4 <= ACC_FULL_WIDTH_BYTES else min(1024, d)
    tr = min(512, n)
    ex = lambda e: jnp.minimum(e, ne - 1)
    rt = lambda e: jnp.maximum(e - ne, 0)
    return pl.pallas_call(
        functools.partial(_ffn_down_kernel, cap=cap, row0=row0, ne=ne, tr=tr),
        grid_spec=pltpu.PrefetchScalarGridSpec(
            num_scalar_prefetch=1,
            grid=(bsz, d // cb, ne + n // tr),
            in_specs=[pl.BlockSpec((None, None, rows, ff), lambda b, c, e, idx: (b, ex(e), 0, 0)),
                      pl.BlockSpec((None, None, ff, cb), lambda b, c, e, idx: (layer, ex(e), 0, c)),
                      pl.BlockSpec((None, None, cap, 1), lambda b, c, e, idx: (b, ex(e), 0, 0)),
                      pl.BlockSpec((None, tr, cb), lambda b, c, e, idx: (b, rt(e), c)),
                      pl.BlockSpec((None, 1, cb), lambda b, c, e, idx: (row_fn(b), 0, c))],
            out_specs=pl.BlockSpec((None, tr, cb), lambda b, c, e, idx: (b, rt(e), c)),
            scratch_shapes=[pltpu.VMEM((n, cb), F32), pltpu.VMEM((cap, cb), F32)]),
        out_shape=jax.ShapeDtypeStruct((bsz, n, d), F32),
        compiler_params=_cparams(("parallel", "parallel", "arbitrary"), 48),
    )(idx_flat, hid, wd_all, gates, x, g2)


def _route(x, norm_g, shift, scale, row_fn, w_router):
    n = x.shape[1]
    cap = EC_CAPACITY * n // w_router.shape[1]
    h, afft, aff = _router(x, norm_g, shift, scale, row_fn, w_router)
    idx, gates = _select(afft, aff, cap)
    return h, idx.reshape(-1), gates, cap


def _expert_ffn_residual(streams, norm_g, shift, scale, g2, w_router, wg, wu, wd):
    routed = [_route(x, norm_g, shift, scale, row_fn, w_router) for x, row_fn in streams]
    hid = _ffn_up([(idx, h, cap) for h, idx, _, cap in routed], wg, wu)
    outs, row0 = [], 0
    for (x, row_fn), (_, idx, gates, cap) in zip(streams, routed):
        outs.append(_ffn_down(idx, hid, row0, wd, gates, x, g2, row_fn))
        row0 += cap
    return outs


def kernel(x, c, ctx, c_ctx, w_ada, b_ada, norm1_g, norm2_g, w_in, w_out, a_lambda, a_subln_g, b_q_norm_g,
           b_k_norm_g, c_rpb, w_router, w_e_gate, w_e_up, w_e_down, final_g):
    bsz, n, d = x.shape
    n_ctx = ctx.shape[1]
    depth = w_ada.shape[0]
    rows = n // GRID_W
    unit_heads = d // 8 // HEAD_DIM
    a_heads, b_heads, c_heads = 2 * unit_heads, 3 * unit_heads, 3 * unit_heads
    assert bsz + 1 <= MOD_ROWS and rows >= 1 and n % GRID_W == 0

    cvec = jnp.zeros((MOD_ROWS, d), F32).at[:bsz].set(c).at[bsz].set(c_ctx)
    mod = _ada_mod(cvec, w_ada, b_ada)
    lat_row = lambda b: b
    ctx_row = lambda b: bsz

    tabs_a, tabs_b = _rope_tables(n, "A"), _rope_tables(n, "B")
    na_toep = _na_toeplitz(c_rpb)

    for l in range(depth):
        need_ctx = l < depth - 1
        lam_init = 0.8 - 0.6 * float(np.exp(-0.3 * l))
        m6 = [mod[l, :, k * d:(k + 1) * d].reshape(MOD_ROWS, 1, d) for k in range(N_MOD)]
        sh1, sc1, g1, sh2, sc2, g2 = m6
        w_in_l, w_out_l = (w_in, l), (w_out, l)
        wg, wu, wd = (w_e_gate, l), (w_e_up, l), (w_e_down, l)

        hx = _norm_mod(x, norm1_g[l], sh1, sc1, lat_row, BF16).reshape(bsz * n, d)
        hc = _norm_mod(ctx, norm1_g[l], sh1, sc1, ctx_row, BF16).reshape(bsz * n_ctx, d)
        px, pc = _in_proj(hx, hc, w_in_l, tabs_a, tabs_b, b_q_norm_g[l], b_k_norm_g[l], n)

        common = dict(n_ctx=n_ctx, n_lat=n, unit_heads=unit_heads)
        a_args = dict(kv_heads=a_heads, group=1, q_unit=UNIT_AQ, k_unit=UNIT_AK, v_unit=UNIT_AV, diff=True,
                      lam=a_lambda[l], subln_g=a_subln_g[l], lam_init=lam_init)
        b_args = dict(kv_heads=unit_heads, group=b_heads // unit_heads, q_unit=UNIT_BQ, k_unit=UNIT_BK, v_unit=UNIT_BV)
        oa = _attention(px, pc, px, n_q=n, **a_args, **common)
        ob = _attention(px, pc, px, n_q=n, **b_args, **common)
        oc = _neighbourhood(px, pc, na_toep[l], n_lat=n, n_ctx=n_ctx, heads=c_heads, unit_heads=unit_heads)
        x = _out_proj(oa, ob, oc, w_out_l, x, g1, lat_row)
        streams = [(x, lat_row)]
        if need_ctx:
            c_args = dict(kv_heads=c_heads, group=1, q_unit=UNIT_CQ, k_unit=UNIT_CK, v_unit=UNIT_CV)
            oa_c = _attention(pc, pc, None, n_q=n_ctx, **a_args, **common)
            ob_c = _attention(pc, pc, None, n_q=n_ctx, **b_args, **common)
            oc_c = _attention(pc, pc, None, n_q=n_ctx, **c_args, **common)
            ctx = _out_proj(oa_c, ob_c, oc_c, w_out_l, ctx, g1, ctx_row)
            streams.append((ctx, ctx_row))
        outs = _expert_ffn_residual(streams, norm2_g[l], sh2, sc2, g2, w_router[l], wg, wu, wd)
        x = outs[0]
        if need_ctx:
            ctx = outs[1]

    return _final_norm(x, final_g)
```

```python
import functools

import numpy as np
import jax
import jax.numpy as jnp
from jax import lax
from jax.experimental import pallas as pl
from jax.experimental.pallas import tpu as pltpu

HEAD_DIM = 128
A_QK_DIM = HEAD_DIM // 2
GRID_W = 64
NA_ROWS = 8
NA_COLS = 16
EC_CAPACITY = 2
N_MOD = 6
ROPE_BASE = 10000.0
EPS = 1e-6
NEG = -1e30
LOG2E = 1.4426950408889634
A_Q_SCALE = A_QK_DIM ** -0.5 * LOG2E
Q_SCALE = HEAD_DIM ** -0.5 * LOG2E
MIB = 1024 * 1024
MOD_ROWS = 8
F32 = jnp.float32
BF16 = jnp.bfloat16

UNIT_AQ, UNIT_AK, UNIT_AV = 0, 2, 4
UNIT_BQ, UNIT_BK, UNIT_BV = 6, 9, 10
UNIT_CQ, UNIT_CK, UNIT_CV = 11, 14, 17
N_UNITS = 20


def _cparams(sem, vmem_mib):
    return pltpu.CompilerParams(dimension_semantics=sem, vmem_limit_bytes=vmem_mib * MIB)


def _nt_dot(a, b):
    return lax.dot_general(a, b, (((1,), (1,)), ((), ())), preferred_element_type=F32)


def _ada_kernel(c_ref, w_ref, b_ref, o_ref):
    c = c_ref[...]
    s = (c * jax.nn.sigmoid(c)).astype(BF16)
    o_ref[...] = jnp.dot(s, w_ref[...].astype(BF16), preferred_element_type=F32) + b_ref[...]


def _ada_mod(cvec, w_ada, b_ada):
    depth, d, nm = w_ada.shape
    tn = min(512, nm)
    return pl.pallas_call(
        _ada_kernel,
        grid=(depth, nm // tn),
        in_specs=[pl.BlockSpec((MOD_ROWS, d), lambda l, j: (0, 0)),
                  pl.BlockSpec((None, d, tn), lambda l, j: (l, 0, j)),
                  pl.BlockSpec((None, 1, tn), lambda l, j: (l, 0, j))],
        out_specs=pl.BlockSpec((None, MOD_ROWS, tn), lambda l, j: (l, 0, j)),
        out_shape=jax.ShapeDtypeStruct((depth, MOD_ROWS, nm), F32),
        compiler_params=_cparams(("parallel", "parallel"), 40),
    )(cvec, w_ada, b_ada.reshape(depth, 1, nm))


def _rms(x, g):
    return x * lax.rsqrt(jnp.mean(x * x, axis=-1, keepdims=True) + EPS) * g


def _norm_mod_kernel(x_ref, g_ref, sh_ref, sc_ref, o_ref):
    y = _rms(x_ref[...], g_ref[...])
    o_ref[...] = (y * (1.0 + sc_ref[...]) + sh_ref[...]).astype(o_ref.dtype)


def _norm_kernel(x_ref, g_ref, o_ref):
    o_ref[...] = _rms(x_ref[...], g_ref[...]).astype(o_ref.dtype)


def _row_spec(d, row_fn):
    return pl.BlockSpec((None, 1, d), lambda b, i: (row_fn(b), 0, 0))


def _norm_mod(x, g, shift, scale, row_fn, out_dtype):
    bsz, n, d = x.shape
    tm = min(256, n)
    return pl.pallas_call(
        _norm_mod_kernel,
        grid=(bsz, n // tm),
        in_specs=[pl.BlockSpec((None, tm, d), lambda b, i: (b, i, 0)),
                  pl.BlockSpec((1, d), lambda b, i: (0, 0)),
                  _row_spec(d, row_fn), _row_spec(d, row_fn)],
        out_specs=pl.BlockSpec((None, tm, d), lambda b, i: (b, i, 0)),
        out_shape=jax.ShapeDtypeStruct((bsz, n, d), out_dtype),
        compiler_params=_cparams(("parallel", "parallel"), 32),
    )(x, g.reshape(1, d), shift, scale)


def _final_norm(x, g):
    bsz, n, d = x.shape
    tm = min(256, n)
    return pl.pallas_call(
        _norm_kernel,
        grid=(bsz, n // tm),
        in_specs=[pl.BlockSpec((None, tm, d), lambda b, i: (b, i, 0)),
                  pl.BlockSpec((1, d), lambda b, i: (0, 0))],
        out_specs=pl.BlockSpec((None, tm, d), lambda b, i: (b, i, 0)),
        out_shape=jax.ShapeDtypeStruct((bsz, n, d), x.dtype),
        compiler_params=_cparams(("parallel", "parallel"), 32),
    )(x, g.reshape(1, d))


def _rope_tables(n, group):
    p = jnp.arange(HEAD_DIM)
    half = 16 if group == "A" else 32
    use_col = ((p // (2 * half)) % 2) == 1
    first = (p % (2 * half)) < half
    freqs = ROPE_BASE ** (-(p % half).astype(F32) / half)
    t = jnp.arange(n)
    pos = jnp.where(use_col[None, :], (t % GRID_W)[:, None], (t // GRID_W)[:, None]).astype(F32)
    ang = pos * freqs[None, :]
    return jnp.cos(ang), jnp.where(first[None, :], -jnp.sin(ang), jnp.sin(ang))


def _rotate(x, cos, sin_signed, half):
    lane = lax.broadcasted_iota(jnp.int32, x.shape, 1)
    first = (lane % (2 * half)) < half
    partner = jnp.where(first, pltpu.roll(x, HEAD_DIM - half, 1), pltpu.roll(x, half, 1))
    return x * cos + partner * sin_signed


INPROJ_SUBTILES = 2


def _inproj_emit(j, h_ref, wb_ref, rope_a, rope_b, gq_ref, gk_ref, o_ref):
    tm = h_ref.shape[0]
    sub = tm // INPROJ_SUBTILES if tm % (16 * INPROJ_SUBTILES) == 0 else tm
    heads = [slice(i * HEAD_DIM, (i + 1) * HEAD_DIM) for i in range(wb_ref.shape[1] // HEAD_DIM)]

    def tiles():
        for r in range(0, tm, sub):
            rows = slice(r, r + sub)
            yield rows, jnp.dot(h_ref[rows, :], wb_ref[...], preferred_element_type=F32)

    def rot(x, tabs, half, rows):
        return x if tabs is None else _rotate(x, tabs[0][rows, :], tabs[1][rows, :], half)

    @pl.when(j < UNIT_AV)
    def _():
        scale = jnp.where(j < UNIT_AK, A_Q_SCALE, 1.0).astype(F32)
        for rows, acc in tiles():
            for s in heads:
                o_ref[rows, s] = (rot(acc[:, s], rope_a, 16, rows) * scale).astype(o_ref.dtype)

    @pl.when((j >= UNIT_BQ) & (j < UNIT_BV))
    def _():
        is_q = j < UNIT_BK
        g = jnp.where(is_q, gq_ref[...], gk_ref[...])
        scale = jnp.where(is_q, Q_SCALE, 1.0).astype(F32)
        for rows, acc in tiles():
            for s in heads:
                o_ref[rows, s] = (rot(_rms(acc[:, s], g), rope_b, 32, rows) * scale).astype(o_ref.dtype)

    @pl.when((j >= UNIT_CQ) & (j < UNIT_CK))
    def _():
        for rows, acc in tiles():
            o_ref[rows, :] = (acc * Q_SCALE).astype(o_ref.dtype)

    @pl.when(((j >= UNIT_AV) & (j < UNIT_BQ)) | (j == UNIT_BV) | (j >= UNIT_CK))
    def _():
        for rows, acc in tiles():
            o_ref[rows, :] = acc.astype(o_ref.dtype)


def _inproj_kernel(hx_ref, hc_ref, w_ref, ca_ref, sa_ref, cb_ref, sb_ref, gq_ref, gk_ref, ox_ref, oc_ref, wb_ref, *, nt):
    j, i = pl.program_id(0), pl.program_id(1)

    @pl.when(i == 0)
    def _():
        wb_ref[...] = w_ref[...].astype(BF16)

    @pl.when(i < nt)
    def _():
        _inproj_emit(j, hx_ref, wb_ref, (ca_ref, sa_ref), (cb_ref, sb_ref), gq_ref, gk_ref, ox_ref)

    @pl.when(i == nt)
    def _():
        _inproj_emit(j, hc_ref, wb_ref, None, None, gq_ref, gk_ref, oc_ref)


def _in_proj(hx, hc, w, tabs_a, tabs_b, gq, gk, n_per_sample):
    w_all, layer = w
    m, d = hx.shape
    mc = hc.shape[0]
    wid = w_all.shape[2]
    tn = wid // N_UNITS
    tm = min(1024, n_per_sample)
    tiles = n_per_sample // tm
    nt = m // tm
    lat = lambda i: jnp.minimum(i, nt - 1)
    tab = pl.BlockSpec((tm, HEAD_DIM), lambda j, i: (lat(i) % tiles, 0))
    vec = pl.BlockSpec((1, HEAD_DIM), lambda j, i: (0, 0))
    return pl.pallas_call(
        functools.partial(_inproj_kernel, nt=nt),
        grid=(N_UNITS, nt + 1),
        in_specs=[pl.BlockSpec((tm, d), lambda j, i: (lat(i), 0)),
                  pl.BlockSpec((mc, d), lambda j, i: (0, 0)),
                  pl.BlockSpec((None, d, tn), lambda j, i: (layer, 0, j)),
                  tab, tab, tab, tab, vec, vec],
        out_specs=[pl.BlockSpec((tm, tn), lambda j, i: (lat(i), j)),
                   pl.BlockSpec((mc, tn), lambda j, i: (0, j))],
        out_shape=[jax.ShapeDtypeStruct((m, wid), BF16), jax.ShapeDtypeStruct((mc, wid), BF16)],
        scratch_shapes=[pltpu.VMEM((d, tn), BF16)],
        compiler_params=_cparams(("arbitrary", "arbitrary"), 56),
    )(hx, hc, w_all, tabs_a[0], tabs_a[1], tabs_b[0], tabs_b[1], gq.reshape(1, HEAD_DIM), gk.reshape(1, HEAD_DIM))


ATTN_TILE_ROWS = 1024
ATTN_SUB_ROWS = 256


def _attn_kernel(*refs, diff, has_lat, group, lam_init):
    it = iter(refs)
    q_ref, kc_ref, vc_ref = next(it), next(it), next(it)
    kl_ref = vl_ref = lam_ref = g_ref = None
    if has_lat:
        kl_ref, vl_ref = next(it), next(it)
    if diff:
        lam_ref, g_ref = next(it), next(it)
    o_ref, k_scr, v_scr = next(it), next(it), next(it)
    n_ctx = kc_ref.shape[0]

    @pl.when(pl.program_id(2) == 0)
    def _():
        k_scr[0:n_ctx, :] = kc_ref[...]
        v_scr[0:n_ctx, 0:HEAD_DIM] = vc_ref[...]
        if has_lat:
            k_scr[n_ctx:, :] = kl_ref[...]
            v_scr[n_ctx:, 0:HEAD_DIM] = vl_ref[...]
        v_scr[:, HEAD_DIM:] = jnp.ones((v_scr.shape[0], HEAD_DIM), BF16)

    k, v = k_scr[...], v_scr[...]

    def softmax_pv(q):
        s = _nt_dot(q, k)
        p = jnp.exp2(s - jnp.max(s, axis=-1, keepdims=True)).astype(BF16)
        oe = jnp.dot(p, v, preferred_element_type=F32)
        return oe[:, :HEAD_DIM] * (1.0 / oe[:, HEAD_DIM:])

    sub = min(ATTN_SUB_ROWS, q_ref.shape[0])
    for r0 in range(0, q_ref.shape[0], sub):
        rows = slice(r0, r0 + sub)
        for g in range(group):
            cols = slice(g * HEAD_DIM, (g + 1) * HEAD_DIM)
            q = q_ref[rows, cols]
            if diff:
                qf = q.astype(F32)
                lane = lax.broadcasted_iota(jnp.int32, qf.shape, 1)
                o1 = softmax_pv(jnp.where(lane < A_QK_DIM, qf, 0.0).astype(BF16))
                o2 = softmax_pv(jnp.where(lane >= A_QK_DIM, qf, 0.0).astype(BF16))
                lv = lam_ref[...]
                lam = (jnp.exp(jnp.sum(lv[0:1] * lv[1:2], axis=-1, keepdims=True))
                       - jnp.exp(jnp.sum(lv[2:3] * lv[3:4], axis=-1, keepdims=True)) + lam_init)
                o = _rms(o1 - lam * o2, g_ref[...]) * (1.0 - lam_init)
            else:
                o = softmax_pv(q)
            o_ref[rows, cols] = o.astype(o_ref.dtype)


def _attention(pq, pc, px, *, n_q, n_ctx, n_lat, kv_heads, group, q_unit, k_unit, v_unit, unit_heads,
               diff=False, lam=None, subln_g=None, lam_init=0.0):
    bsz = pq.shape[0] // n_q
    has_lat = px is not None
    tq = min(ATTN_TILE_ROWS * (2 if diff else 1), n_q)
    nqt = n_q // tq
    gw = group * HEAD_DIM
    n_keys = n_ctx + (n_lat if has_lat else 0)
    qc, kc, vc = q_unit * unit_heads // group, k_unit * unit_heads, v_unit * unit_heads
    assert (q_unit * unit_heads) % group == 0
    in_specs = [pl.BlockSpec((tq, gw), lambda b, h, i: (b * nqt + i, qc + h)),
                pl.BlockSpec((n_ctx, HEAD_DIM), lambda b, h, i: (b, kc + h)),
                pl.BlockSpec((n_ctx, HEAD_DIM), lambda b, h, i: (b, vc + h))]
    args = [pq, pc, pc]
    if has_lat:
        in_specs += [pl.BlockSpec((n_lat, HEAD_DIM), lambda b, h, i: (b, kc + h)),
                     pl.BlockSpec((n_lat, HEAD_DIM), lambda b, h, i: (b, vc + h))]
        args += [px, px]
    if diff:
        in_specs += [pl.BlockSpec((4, A_QK_DIM), lambda b, h, i: (0, 0)),
                     pl.BlockSpec((1, HEAD_DIM), lambda b, h, i: (0, 0))]
        args += [lam, subln_g.reshape(1, HEAD_DIM)]
    return pl.pallas_call(
        functools.partial(_attn_kernel, diff=diff, has_lat=has_lat, group=group, lam_init=lam_init),
        grid=(bsz, kv_heads, nqt),
        in_specs=in_specs,
        out_specs=pl.BlockSpec((tq, gw), lambda b, h, i: (b * nqt + i, h)),
        out_shape=jax.ShapeDtypeStruct((bsz * n_q, kv_heads * gw), BF16),
        scratch_shapes=[pltpu.VMEM((n_keys, HEAD_DIM), BF16), pltpu.VMEM((n_keys, 2 * HEAD_DIM), BF16)],
        compiler_params=_cparams(("parallel", "parallel", "arbitrary"), 48),
    )(*args)


NA_QROWS = 4


def _na_plan(rows):
    kh = min(NA_ROWS, rows)
    qb = min(NA_QROWS, rows)
    assert rows % qb == 0
    kw = min(rows, kh + qb)
    starts, var_ids, variants = [], [], []
    for r0 in range(0, rows, qb):
        rs = [int(np.clip(r - kh // 2, 0, rows - kh)) for r in range(r0, r0 + qb)]
        kstart = min(min(rs), rows - kw)
        assert kstart >= 0 and max(rs) + kh <= kstart + kw
        sig = tuple((r0 + u - kstart, rs[u] - kstart) for u in range(qb))
        if sig not in variants:
            variants.append(sig)
        starts.append(kstart)
        var_ids.append(variants.index(sig))
    return kw, starts, var_ids, variants


def _na_toeplitz(c_rpb):
    q = np.arange(GRID_W)
    kc = np.arange(GRID_W)
    start = np.clip(q - NA_COLS // 2, 0, GRID_W - NA_COLS)
    inwin = (kc[None, :] >= start[:, None]) & (kc[None, :] < start[:, None] + NA_COLS)
    ci = kc[None, :] - q[:, None] + NA_COLS - 1
    onehot = ((ci[..., None] == np.arange(2 * NA_COLS - 1)) & inwin[..., None]).astype(np.float32)
    toep = jnp.einsum("lhab,qcb->lhaqc", c_rpb.astype(F32), jnp.asarray(onehot), precision=lax.Precision.HIGHEST)
    return jnp.where(jnp.asarray(inwin)[None, None, None], toep * LOG2E, NEG)


def _na_kernel(kst_ref, var_ref, q_ref, k_ref, v_ref, kc_ref, vc_ref, toep_ref, o_ref, t_ref, va_ref, vca_ref, *,
               nblk, qlen, klen, variants, kh):
    kctx = kc_ref[...]
    va_ref[:, 0:HEAD_DIM] = v_ref[...]
    va_ref[:, HEAD_DIM:] = jnp.ones((va_ref.shape[0], HEAD_DIM), BF16)
    vca_ref[:, 0:HEAD_DIM] = vc_ref[...]
    vca_ref[:, HEAD_DIM:] = jnp.ones((vca_ref.shape[0], HEAD_DIM), BF16)
    vctx = vca_ref[...]

    masked = jnp.full((GRID_W, GRID_W), NEG, F32)
    for vi, sig in enumerate(variants):
        for u, (qoff, woff) in enumerate(sig):
            tile = lambda kr: toep_ref[kr - qoff + NA_ROWS - 1] if woff <= kr < woff + kh else masked
            for kr in range(0, klen // GRID_W, 2):
                t_ref[vi, u * GRID_W:(u + 1) * GRID_W, kr * GRID_W:(kr + 2) * GRID_W] = jnp.concatenate(
                    [tile(kr), tile(kr + 1)], axis=1)

    def body(rb, carry):
        qoff = pl.multiple_of(rb * qlen, qlen)
        koff = pl.multiple_of(kst_ref[rb] * GRID_W, GRID_W)
        q = q_ref[pl.ds(qoff, qlen), :]
        s_loc = _nt_dot(q, k_ref[pl.ds(koff, klen), :]) + t_ref[var_ref[rb]]
        s_ctx = _nt_dot(q, kctx)
        m = jnp.maximum(jnp.max(s_loc, axis=-1, keepdims=True), jnp.max(s_ctx, axis=-1, keepdims=True))
        p_loc = jnp.exp2(s_loc - m).astype(BF16)
        p_ctx = jnp.exp2(s_ctx - m).astype(BF16)
        oe = (jnp.dot(p_loc, va_ref[pl.ds(koff, klen), :], preferred_element_type=F32)
              + jnp.dot(p_ctx, vctx, preferred_element_type=F32))
        o_ref[pl.ds(qoff, qlen), :] = (oe[:, :HEAD_DIM] * (1.0 / oe[:, HEAD_DIM:])).astype(o_ref.dtype)
        return carry

    lax.fori_loop(0, nblk, body, 0, unroll=4 if nblk % 4 == 0 else 1)


def _neighbourhood(px, pc, toep, *, n_lat, n_ctx, heads, unit_heads):
    bsz = px.shape[0] // n_lat
    rows = n_lat // GRID_W
    kw, starts, var_ids, variants = _na_plan(rows)
    assert kw % 2 == 0
    nblk = len(starts)
    qlen, klen = n_lat // nblk, kw * GRID_W
    qc, kc, vc = UNIT_CQ * unit_heads, UNIT_CK * unit_heads, UNIT_CV * unit_heads
    lat = lambda c0: pl.BlockSpec((n_lat, HEAD_DIM), lambda b, h, ks, vi: (b, c0 + h))
    ctx = lambda c0: pl.BlockSpec((n_ctx, HEAD_DIM), lambda b, h, ks, vi: (b, c0 + h))
    return pl.pallas_call(
        functools.partial(_na_kernel, nblk=nblk, qlen=qlen, klen=klen, variants=tuple(variants),
                          kh=min(NA_ROWS, rows)),
        grid_spec=pltpu.PrefetchScalarGridSpec(
            num_scalar_prefetch=2,
            grid=(bsz, heads),
            in_specs=[lat(qc), lat(kc), lat(vc), ctx(kc), ctx(vc),
                      pl.BlockSpec((None,) + toep.shape[1:], lambda b, h, ks, vi: (h, 0, 0, 0))],
            out_specs=pl.BlockSpec((n_lat, HEAD_DIM), lambda b, h, ks, vi: (b, h)),
            scratch_shapes=[pltpu.VMEM((len(variants), qlen, klen), F32),
                            pltpu.VMEM((n_lat, 2 * HEAD_DIM), BF16), pltpu.VMEM((n_ctx, 2 * HEAD_DIM), BF16)]),
        out_shape=jax.ShapeDtypeStruct((bsz * n_lat, heads * HEAD_DIM), BF16),
        compiler_params=_cparams(("parallel", "parallel"), 40),
    )(jnp.asarray(starts, jnp.int32), jnp.asarray(var_ids, jnp.int32), px, px, px, pc, pc, toep)


def _outproj_kernel(oa_ref, ob_ref, oc_ref, w_ref, x_ref, g_ref, o_ref, wb_ref):
    @pl.when((pl.program_id(1) == 0) & (pl.program_id(2) == 0))
    def _():
        wb_ref[...] = w_ref[...].astype(BF16)

    wa, wb = oa_ref.shape[1], ob_ref.shape[1]
    acc = jnp.dot(oa_ref[...], wb_ref[0:wa, :], preferred_element_type=F32)
    acc += jnp.dot(ob_ref[...], wb_ref[wa:wa + wb, :], preferred_element_type=F32)
    acc += jnp.dot(oc_ref[...], wb_ref[wa + wb:, :], preferred_element_type=F32)
    o_ref[...] = x_ref[...] + g_ref[...] * acc


def _out_proj(oa, ob, oc, w, x, gate, row_fn):
    w_all, layer = w
    kdim = w_all.shape[1]
    bsz, n, d = x.shape
    tm = min(1024, n)
    tn = min(512, d)
    nt = n // tm
    lhs = lambda a: pl.BlockSpec((tm, a.shape[1]), lambda j, b, i: (b * nt + i, 0))
    return pl.pallas_call(
        _outproj_kernel,
        grid=(d // tn, bsz, nt),
        in_specs=[lhs(oa), lhs(ob), lhs(oc),
                  pl.BlockSpec((None, kdim, tn), lambda j, b, i: (layer, 0, j)),
                  pl.BlockSpec((None, tm, tn), lambda j, b, i: (b, i, j)),
                  pl.BlockSpec((None, 1, tn), lambda j, b, i: (row_fn(b), 0, j))],
        out_specs=pl.BlockSpec((None, tm, tn), lambda j, b, i: (b, i, j)),
        out_shape=jax.ShapeDtypeStruct((bsz, n, d), F32),
        scratch_shapes=[pltpu.VMEM((kdim, tn), BF16)],
        compiler_params=_cparams(("arbitrary", "arbitrary", "arbitrary"), 56),
    )(oa, ob, oc, w_all, x, gate)


def _split3(x):
    hi = x.astype(BF16)
    r = x - hi.astype(F32)
    mid = r.astype(BF16)
    lo = (r - mid.astype(F32)).astype(BF16)
    return hi, mid, lo


def _router_kernel(x_ref, g_ref, sh_ref, sc_ref, wh_ref, wl_ref, h_ref, afft_ref, aff_ref):
    y = _rms(x_ref[...], g_ref[...])
    h = y * (1.0 + sc_ref[...]) + sh_ref[...]
    h_ref[...] = h
    hh = h.astype(BF16)
    hl = (h - hh.astype(F32)).astype(BF16)
    wh, wl = wh_ref[...], wl_ref[...]
    lt = _nt_dot(wh, hh) + _nt_dot(wh, hl) + _nt_dot(wl, hh)
    e = jnp.exp(lt - jnp.max(lt, axis=0, keepdims=True))
    afft_ref[...] = e / jnp.sum(e, axis=0, keepdims=True)
    ln = _nt_dot(hh, wh) + _nt_dot(hl, wh) + _nt_dot(hh, wl)
    en = jnp.exp(ln - jnp.max(ln, axis=-1, keepdims=True))
    aff_ref[...] = en / jnp.sum(en, axis=-1, keepdims=True)


def _router(x, g, shift, scale, row_fn, w_router):
    bsz, n, d = x.shape
    ne = w_router.shape[1]
    tm = min(256, n)
    wt = w_router.T
    wh = wt.astype(BF16)
    wl = (wt - wh.astype(F32)).astype(BF16)
    wspec = pl.BlockSpec((ne, d), lambda b, i: (0, 0))
    return pl.pallas_call(
        _router_kernel,
        grid=(bsz, n // tm),
        in_specs=[pl.BlockSpec((None, tm, d), lambda b, i: (b, i, 0)),
                  pl.BlockSpec((1, d), lambda b, i: (0, 0)),
                  _row_spec(d, row_fn), _row_spec(d, row_fn), wspec, wspec],
        out_specs=[pl.BlockSpec((None, tm, d), lambda b, i: (b, i, 0)),
                   pl.BlockSpec((None, ne, tm), lambda b, i: (b, 0, i)),
                   pl.BlockSpec((None, tm, ne), lambda b, i: (b, i, 0))],
        out_shape=[jax.ShapeDtypeStruct((bsz, n, d), F32),
                   jax.ShapeDtypeStruct((bsz, ne, n), F32),
                   jax.ShapeDtypeStruct((bsz, n, ne), F32)],
        compiler_params=_cparams(("parallel", "parallel"), 40),
    )(x, g.reshape(1, d), shift, scale, wh, wl)


def _lane_cumsum(x01, tri):
    r, n = x01.shape
    carry = jnp.zeros((r, 1), F32)
    out = []
    for c in range(n // HEAD_DIM):
        blk = x01[:, c * HEAD_DIM:(c + 1) * HEAD_DIM].astype(BF16)
        cs = jnp.dot(blk, tri, preferred_element_type=F32) + carry
        out.append(cs)
        carry = cs[:, HEAD_DIM - 1:HEAD_DIM]
    return jnp.concatenate(out, axis=1) if len(out) > 1 else out[0]


def _select_kernel(afft_ref, r_ref, o_ref, *, cap):
    a = afft_ref[...]
    ne, n = a.shape
    bits = pltpu.bitcast(a, jnp.int32)

    def bisect(_, lohi):
        lo, hi = lohi
        mid = lo + lax.shift_right_logical(hi - lo, 1)
        cnt = jnp.sum((bits >= mid).astype(F32), axis=1, keepdims=True)
        ok = cnt >= cap
        return jnp.where(ok, mid, lo), jnp.where(ok, hi, mid)

    lo0 = jnp.zeros((ne, 1), jnp.int32)
    hi0 = jnp.full((ne, 1), 0x7F800000, jnp.int32)
    thr, _ = lax.fori_loop(0, 32, bisect, (lo0, hi0))

    ri = lax.broadcasted_iota(jnp.int32, (HEAD_DIM, HEAD_DIM), 0)
    ci = lax.broadcasted_iota(jnp.int32, (HEAD_DIM, HEAD_DIM), 1)
    tri = (ri <= ci).astype(BF16)
    gt = (bits > thr).astype(F32)
    eq = (bits == thr).astype(F32)
    need = cap - jnp.sum(gt, axis=1, keepdims=True)
    sel = gt + eq * (_lane_cumsum(eq, tri) <= need).astype(F32)
    pos = _lane_cumsum(sel, tri) - 1.0

    slot = lax.broadcasted_iota(jnp.int32, (cap, n), 0).astype(F32)
    rmat = r_ref[...]
    for e in range(ne):
        onehot = jnp.where((pos[e:e + 1, :] == slot) & (sel[e:e + 1, :] > 0.5), 1.0, 0.0).astype(BF16)
        o_ref[e] = jnp.dot(onehot, rmat, preferred_element_type=F32)


IDX_RADIX = 64


def _select(afft, aff, cap):
    bsz, ne, n = afft.shape
    t = jnp.arange(n)
    hi, mid, lo = _split3(aff)
    assert n <= IDX_RADIX * 256
    tcols = jnp.stack([t // IDX_RADIX, t % IDX_RADIX], axis=1).astype(BF16)
    pad = jnp.zeros((bsz, n, HEAD_DIM - 2 - 3 * ne), BF16)
    rmat = jnp.concatenate([jnp.broadcast_to(tcols[None], (bsz, n, 2)), hi, mid, lo, pad], axis=2)
    tab = pl.pallas_call(
        functools.partial(_select_kernel, cap=cap),
        grid=(bsz,),
        in_specs=[pl.BlockSpec((None, ne, n), lambda b: (b, 0, 0)),
                  pl.BlockSpec((None, n, HEAD_DIM), lambda b: (b, 0, 0))],
        out_specs=pl.BlockSpec((None, ne, cap, HEAD_DIM), lambda b: (b, 0, 0, 0)),
        out_shape=jax.ShapeDtypeStruct((bsz, ne, cap, HEAD_DIM), F32),
        compiler_params=_cparams(("parallel",), 40),
    )(afft, rmat)
    idx = (tab[..., 0] * float(IDX_RADIX) + tab[..., 1]).astype(jnp.int32)
    parts = tab[..., 2:2 + 3 * ne].reshape(bsz, ne, cap, 3, ne).sum(axis=3)
    gates = (parts * jnp.eye(ne, dtype=F32)[None, :, None, :]).sum(axis=-1, keepdims=True)
    return idx, gates


def _ffn_up_kernel(*refs, caps, ne, nb):
    ns = len(caps)
    idx_refs, h_refs = refs[:ns], refs[ns:2 * ns]
    wg_ref, wu_ref, o_ref, xg_ref, sem = refs[2 * ns:]
    rows = sum(caps)
    e, b = pl.program_id(0), pl.program_id(1)
    step = e * nb + b
    slot = lax.rem(step, 2)

    def gather(e_, b_, slot_):
        r0 = 0
        for idx_ref, h_hbm, cap in zip(idx_refs, h_refs, caps):
            base = (b_ * ne + e_) * cap
            for j in range(cap):
                pltpu.make_async_copy(h_hbm.at[b_, pl.ds(idx_ref[base + j], 1), :],
                                      xg_ref.at[slot_, pl.ds(r0 + j, 1), :], sem.at[slot_]).start()
            r0 += cap

    def wait_all(slot_):
        pltpu.make_async_copy(xg_ref.at[slot_], xg_ref.at[slot_], sem.at[slot_]).wait()

    @pl.when(step == 0)
    def _():
        gather(e, b, 0)

    wait_all(slot)

    last = step + 1 >= ne * nb
    nxt = jnp.where(last, step, step + 1)
    gather(lax.div(nxt, nb), lax.rem(nxt, nb), 1 - slot)

    x = xg_ref[slot].astype(BF16)
    gate = jnp.dot(x, wg_ref[...].astype(BF16), preferred_element_type=F32)
    up = jnp.dot(x, wu_ref[...].astype(BF16), preferred_element_type=F32)
    o_ref[...] = (gate * jax.nn.sigmoid(gate) * up).astype(o_ref.dtype)

    @pl.when(last)
    def _():
        wait_all(1 - slot)


def _ffn_up(streams, wg, wu):
    caps = tuple(c for _, _, c in streams)
    bsz, _, d = streams[0][1].shape
    (wg_all, layer), (wu_all, _) = wg, wu
    ne, _, ff = wg_all.shape[1:]
    ns = len(streams)
    imap = lambda f: (lambda e, b, *idx: f(e, b))
    wspec = pl.BlockSpec((None, None, d, ff), imap(lambda e, b: (layer, e, 0, 0)))
    rows = sum(caps)
    return pl.pallas_call(
        functools.partial(_ffn_up_kernel, caps=caps, ne=ne, nb=bsz),
        grid_spec=pltpu.PrefetchScalarGridSpec(
            num_scalar_prefetch=ns,
            grid=(ne, bsz),
            in_specs=[pl.BlockSpec(memory_space=pl.ANY)] * ns + [wspec, wspec],
            out_specs=pl.BlockSpec((None, None, rows, ff), imap(lambda e, b: (b, e, 0, 0))),
            scratch_shapes=[pltpu.VMEM((2, rows, d), F32), pltpu.SemaphoreType.DMA((2,))]),
        out_shape=jax.ShapeDtypeStruct((bsz, ne, rows, ff), BF16),
        compiler_params=_cparams(("arbitrary", "arbitrary"), 58),
    )(*[i for i, _, _ in streams], *[h for _, h, _ in streams], wg_all, wu_all)


ACC_FULL_WIDTH_BYTES = 8 * MIB
SCATTER_ROWS = 8


def _ffn_down_kernel(idx_ref, hid_ref, wd_ref, gate_ref, x_ref, g2_ref, o_ref, acc_ref, ye_ref, *, cap, row0, ne, tr):
    b, e = pl.program_id(0), pl.program_id(2)

    @pl.when(e == 0)
    def _():
        acc_ref[...] = jnp.zeros_like(acc_ref)

    @pl.when(e < ne)
    def _():
        hid = hid_ref[row0:row0 + cap, :]
        ye_ref[...] = jnp.dot(hid, wd_ref[...].astype(BF16), preferred_element_type=F32) * gate_ref[...]
        base = (b * ne + e) * cap

        for j0 in range(0, cap, SCATTER_ROWS):
            toks = [idx_ref[base + j0 + u] for u in range(SCATTER_ROWS)]
            rows = [acc_ref[pl.ds(t, 1), :] for t in toks]
            for u, (t, r) in enumerate(zip(toks, rows)):
                acc_ref[pl.ds(t, 1), :] = r + ye_ref[j0 + u:j0 + u + 1, :]

    @pl.when(e >= ne)
    def _():
        r0 = pl.multiple_of((e - ne) * tr, tr)
        o_ref[...] = x_ref[...] + g2_ref[...] * acc_ref[pl.ds(r0, tr), :]


def _ffn_down(idx_flat, hid, row0, wd, gates, x, g2, row_fn):
    bsz, n, d = x.shape
    wd_all, layer = wd
    ne, ff = wd_all.shape[1:3]
    cap = gates.shape[2]
    rows = hid.shape[2]
    assert cap % SCATTER_ROWS == 0
    cb = d if n * d * 4 <= ACC_FULL_WIDTH_BYTES else min(1024, d)
    tr = min(512, n)
    ex = lambda e: jnp.minimum(e, ne - 1)
    rt = lambda e: jnp.maximum(e - ne, 0)
    return pl.pallas_call(
        functools.partial(_ffn_down_kernel, cap=cap, row0=row0, ne=ne, tr=tr),
        grid_spec=pltpu.PrefetchScalarGridSpec(
            num_scalar_prefetch=1,
            grid=(bsz, d // cb, ne + n // tr),
            in_specs=[pl.BlockSpec((None, None, rows, ff), lambda b, c, e, idx: (b, ex(e), 0, 0)),
                      pl.BlockSpec((None, None, ff, cb), lambda b, c, e, idx: (layer, ex(e), 0, c)),
                      pl.BlockSpec((None, None, cap, 1), lambda b, c, e, idx: (b, ex(e), 0, 0)),
                      pl.BlockSpec((None, tr, cb), lambda b, c, e, idx: (b, rt(e), c)),
                      pl.BlockSpec((None, 1, cb), lambda b, c, e, idx: (row_fn(b), 0, c))],
            out_specs=pl.BlockSpec((None, tr, cb), lambda b, c, e, idx: (b, rt(e), c)),
            scratch_shapes=[pltpu.VMEM((n, cb), F32), pltpu.VMEM((cap, cb), F32)]),
        out_shape=jax.ShapeDtypeStruct((bsz, n, d), F32),
        compiler_params=_cparams(("parallel", "parallel", "arbitrary"), 48),
    )(idx_flat, hid, wd_all, gates, x, g2)


def _route(x, norm_g, shift, scale, row_fn, w_router):
    n = x.shape[1]
    cap = EC_CAPACITY * n // w_router.shape[1]
    h, afft, aff = _router(x, norm_g, shift, scale, row_fn, w_router)
    idx, gates = _select(afft, aff, cap)
    return h, idx.reshape(-1), gates, cap


def _expert_ffn_residual(streams, norm_g, shift, scale, g2, w_router, wg, wu, wd):
    routed = [_route(x, norm_g, shift, scale, row_fn, w_router) for x, row_fn in streams]
    hid = _ffn_up([(idx, h, cap) for h, idx, _, cap in routed], wg, wu)
    outs, row0 = [], 0
    for (x, row_fn), (_, idx, gates, cap) in zip(streams, routed):
        outs.append(_ffn_down(idx, hid, row0, wd, gates, x, g2, row_fn))
        row0 += cap
    return outs


def kernel(x, c, ctx, c_ctx, w_ada, b_ada, norm1_g, norm2_g, w_in, w_out, a_lambda, a_subln_g, b_q_norm_g,
           b_k_norm_g, c_rpb, w_router, w_e_gate, w_e_up, w_e_down, final_g):
    bsz, n, d = x.shape
    n_ctx = ctx.shape[1]
    depth = w_ada.shape[0]
    rows = n // GRID_W
    unit_heads = d // 8 // HEAD_DIM
    a_heads, b_heads, c_heads = 2 * unit_heads, 3 * unit_heads, 3 * unit_heads
    assert bsz + 1 <= MOD_ROWS and rows >= 1 and n % GRID_W == 0

    cvec = jnp.zeros((MOD_ROWS, d), F32).at[:bsz].set(c).at[bsz].set(c_ctx)
    mod = _ada_mod(cvec, w_ada, b_ada)
    lat_row = lambda b: b
    ctx_row = lambda b: bsz

    tabs_a, tabs_b = _rope_tables(n, "A"), _rope_tables(n, "B")
    na_toep = _na_toeplitz(c_rpb)

    for l in range(depth):
        need_ctx = l < depth - 1
        lam_init = 0.8 - 0.6 * float(np.exp(-0.3 * l))
        m6 = [mod[l, :, k * d:(k + 1) * d].reshape(MOD_ROWS, 1, d) for k in range(N_MOD)]
        sh1, sc1, g1, sh2, sc2, g2 = m6
        w_in_l, w_out_l = (w_in, l), (w_out, l)
        wg, wu, wd = (w_e_gate, l), (w_e_up, l), (w_e_down, l)

        hx = _norm_mod(x, norm1_g[l], sh1, sc1, lat_row, BF16).reshape(bsz * n, d)
        hc = _norm_mod(ctx, norm1_g[l], sh1, sc1, ctx_row, BF16).reshape(bsz * n_ctx, d)
        px, pc = _in_proj(hx, hc, w_in_l, tabs_a, tabs_b, b_q_norm_g[l], b_k_norm_g[l], n)

        common = dict(n_ctx=n_ctx, n_lat=n, unit_heads=unit_heads)
        a_args = dict(kv_heads=a_heads, group=1, q_unit=UNIT_AQ, k_unit=UNIT_AK, v_unit=UNIT_AV, diff=True,
                      lam=a_lambda[l], subln_g=a_subln_g[l], lam_init=lam_init)
        b_args = dict(kv_heads=unit_heads, group=b_heads // unit_heads, q_unit=UNIT_BQ, k_unit=UNIT_BK, v_unit=UNIT_BV)
        oa = _attention(px, pc, px, n_q=n, **a_args, **common)
        ob = _attention(px, pc, px, n_q=n, **b_args, **common)
        oc = _neighbourhood(px, pc, na_toep[l], n_lat=n, n_ctx=n_ctx, heads=c_heads, unit_heads=unit_heads)
        x = _out_proj(oa, ob, oc, w_out_l, x, g1, lat_row)
        streams = [(x, lat_row)]
        if need_ctx:
            c_args = dict(kv_heads=c_heads, group=1, q_unit=UNIT_CQ, k_unit=UNIT_CK, v_unit=UNIT_CV)
            oa_c = _attention(pc, pc, None, n_q=n_ctx, **a_args, **common)
            ob_c = _attention(pc, pc, None, n_q=n_ctx, **b_args, **common)
            oc_c = _attention(pc, pc, None, n_q=n_ctx, **c_args, **common)
            ctx = _out_proj(oa_c, ob_c, oc_c, w_out_l, ctx, g1, ctx_row)
            streams.append((ctx, ctx_row))
        outs = _expert_ffn_residual(streams, norm2_g[l], sh2, sc2, g2, w_router[l], wg, wu, wd)
        x = outs[0]
        if need_ctx:
            ctx = outs[1]

    return _final_norm(x, final_g)
```

```python
import functools

import numpy as np
import jax
import jax.numpy as jnp
from jax import lax
from jax.experimental import pallas as pl
from jax.experimental.pallas import tpu as pltpu

HEAD_DIM = 128
A_QK_DIM = HEAD_DIM // 2
GRID_W = 64
NA_ROWS = 8
NA_COLS = 16
EC_CAPACITY = 2
N_MOD = 6
ROPE_BASE = 10000.0
EPS = 1e-6
NEG = -1e30
LOG2E = 1.4426950408889634
A_Q_SCALE = A_QK_DIM ** -0.5 * LOG2E
Q_SCALE = HEAD_DIM ** -0.5 * LOG2E
MIB = 1024 * 1024
MOD_ROWS = 8
F32 = jnp.float32
BF16 = jnp.bfloat16

UNIT_AQ, UNIT_AK, UNIT_AV = 0, 2, 4
UNIT_BQ, UNIT_BK, UNIT_BV = 6, 9, 10
UNIT_CQ, UNIT_CK, UNIT_CV = 11, 14, 17
N_UNITS = 20


def _cparams(sem, vmem_mib):
    return pltpu.CompilerParams(dimension_semantics=sem, vmem_limit_bytes=vmem_mib * MIB)


def _nt_dot(a, b):
    return lax.dot_general(a, b, (((1,), (1,)), ((), ())), preferred_element_type=F32)


def _ada_kernel(c_ref, w_ref, b_ref, o_ref):
    c = c_ref[...]
    s = (c * jax.nn.sigmoid(c)).astype(BF16)
    o_ref[...] = jnp.dot(s, w_ref[...].astype(BF16), preferred_element_type=F32) + b_ref[...]


def _ada_mod(cvec, w_ada, b_ada):
    depth, d, nm = w_ada.shape
    tn = min(512, nm)
    return pl.pallas_call(
        _ada_kernel,
        grid=(depth, nm // tn),
        in_specs=[pl.BlockSpec((MOD_ROWS, d), lambda l, j: (0, 0)),
                  pl.BlockSpec((None, d, tn), lambda l, j: (l, 0, j)),
                  pl.BlockSpec((None, 1, tn), lambda l, j: (l, 0, j))],
        out_specs=pl.BlockSpec((None, MOD_ROWS, tn), lambda l, j: (l, 0, j)),
        out_shape=jax.ShapeDtypeStruct((depth, MOD_ROWS, nm), F32),
        compiler_params=_cparams(("parallel", "parallel"), 40),
    )(cvec, w_ada, b_ada.reshape(depth, 1, nm))


def _rms(x, g):
    return x * lax.rsqrt(jnp.mean(x * x, axis=-1, keepdims=True) + EPS) * g


def _norm_mod_kernel(x_ref, g_ref, sh_ref, sc_ref, o_ref):
    y = _rms(x_ref[...], g_ref[...])
    o_ref[...] = (y * (1.0 + sc_ref[...]) + sh_ref[...]).astype(o_ref.dtype)


def _norm_kernel(x_ref, g_ref, o_ref):
    o_ref[...] = _rms(x_ref[...], g_ref[...]).astype(o_ref.dtype)


def _row_spec(d, row_fn):
    return pl.BlockSpec((None, 1, d), lambda b, i: (row_fn(b), 0, 0))


def _norm_mod(x, g, shift, scale, row_fn, out_dtype):
    bsz, n, d = x.shape
    tm = min(256, n)
    return pl.pallas_call(
        _norm_mod_kernel,
        grid=(bsz, n // tm),
        in_specs=[pl.BlockSpec((None, tm, d), lambda b, i: (b, i, 0)),
                  pl.BlockSpec((1, d), lambda b, i: (0, 0)),
                  _row_spec(d, row_fn), _row_spec(d, row_fn)],
        out_specs=pl.BlockSpec((None, tm, d), lambda b, i: (b, i, 0)),
        out_shape=jax.ShapeDtypeStruct((bsz, n, d), out_dtype),
        compiler_params=_cparams(("parallel", "parallel"), 32),
    )(x, g.reshape(1, d), shift, scale)


def _final_norm(x, g):
    bsz, n, d = x.shape
    tm = min(256, n)
    return pl.pallas_call(
        _norm_kernel,
        grid=(bsz, n // tm),
        in_specs=[pl.BlockSpec((None, tm, d), lambda b, i: (b, i, 0)),
                  pl.BlockSpec((1, d), lambda b, i: (0, 0))],
        out_specs=pl.BlockSpec((None, tm, d), lambda b, i: (b, i, 0)),
        out_shape=jax.ShapeDtypeStruct((bsz, n, d), x.dtype),
        compiler_params=_cparams(("parallel", "parallel"), 32),
    )(x, g.reshape(1, d))


def _rope_tables(n, group):
    p = jnp.arange(HEAD_DIM)
    half = 16 if group == "A" else 32
    use_col = ((p // (2 * half)) % 2) == 1
    first = (p % (2 * half)) < half
    freqs = ROPE_BASE ** (-(p % half).astype(F32) / half)
    t = jnp.arange(n)
    pos = jnp.where(use_col[None, :], (t % GRID_W)[:, None], (t // GRID_W)[:, None]).astype(F32)
    ang = pos * freqs[None, :]
    return jnp.cos(ang), jnp.where(first[None, :], -jnp.sin(ang), jnp.sin(ang))


def _rotate(x, cos, sin_signed, half):
    lane = lax.broadcasted_iota(jnp.int32, x.shape, 1)
    first = (lane % (2 * half)) < half
    partner = jnp.where(first, pltpu.roll(x, HEAD_DIM - half, 1), pltpu.roll(x, half, 1))
    return x * cos + partner * sin_signed


INPROJ_SUBTILES = 2


def _inproj_emit(j, h_ref, wb_ref, rope_a, rope_b, gq_ref, gk_ref, o_ref):
    tm = h_ref.shape[0]
    sub = tm // INPROJ_SUBTILES if tm % (16 * INPROJ_SUBTILES) == 0 else tm
    heads = [slice(i * HEAD_DIM, (i + 1) * HEAD_DIM) for i in range(wb_ref.shape[1] // HEAD_DIM)]

    def tiles():
        for r in range(0, tm, sub):
            rows = slice(r, r + sub)
            yield rows, jnp.dot(h_ref[rows, :], wb_ref[...], preferred_element_type=F32)

    def rot(x, tabs, half, rows):
        return x if tabs is None else _rotate(x, tabs[0][rows, :], tabs[1][rows, :], half)

    @pl.when(j < UNIT_AV)
    def _():
        scale = jnp.where(j < UNIT_AK, A_Q_SCALE, 1.0).astype(F32)
        for rows, acc in tiles():
            for s in heads:
                o_ref[rows, s] = (rot(acc[:, s], rope_a, 16, rows) * scale).astype(o_ref.dtype)

    @pl.when((j >= UNIT_BQ) & (j < UNIT_BV))
    def _():
        is_q = j < UNIT_BK
        g = jnp.where(is_q, gq_ref[...], gk_ref[...])
        scale = jnp.where(is_q, Q_SCALE, 1.0).astype(F32)
        for rows, acc in tiles():
            for s in heads:
                o_ref[rows, s] = (rot(_rms(acc[:, s], g), rope_b, 32, rows) * scale).astype(o_ref.dtype)

    @pl.when((j >= UNIT_CQ) & (j < UNIT_CK))
    def _():
        for rows, acc in tiles():
            o_ref[rows, :] = (acc * Q_SCALE).astype(o_ref.dtype)

    @pl.when(((j >= UNIT_AV) & (j < UNIT_BQ)) | (j == UNIT_BV) | (j >= UNIT_CK))
    def _():
        for rows, acc in tiles():
            o_ref[rows, :] = acc.astype(o_ref.dtype)


def _inproj_kernel(hx_ref, hc_ref, w_ref, ca_ref, sa_ref, cb_ref, sb_ref, gq_ref, gk_ref, ox_ref, oc_ref, wb_ref, *, nt):
    j, i = pl.program_id(0), pl.program_id(1)

    @pl.when(i == 0)
    def _():
        wb_ref[...] = w_ref[...].astype(BF16)

    @pl.when(i < nt)
    def _():
        _inproj_emit(j, hx_ref, wb_ref, (ca_ref, sa_ref), (cb_ref, sb_ref), gq_ref, gk_ref, ox_ref)

    @pl.when(i == nt)
    def _():
        _inproj_emit(j, hc_ref, wb_ref, None, None, gq_ref, gk_ref, oc_ref)


def _in_proj(hx, hc, w, tabs_a, tabs_b, gq, gk, n_per_sample):
    w_all, layer = w
    m, d = hx.shape
    mc = hc.shape[0]
    wid = w_all.shape[2]
    tn = wid // N_UNITS
    tm = min(1024, n_per_sample)
    tiles = n_per_sample // tm
    nt = m // tm
    lat = lambda i: jnp.minimum(i, nt - 1)
    tab = pl.BlockSpec((tm, HEAD_DIM), lambda j, i: (lat(i) % tiles, 0))
    vec = pl.BlockSpec((1, HEAD_DIM), lambda j, i: (0, 0))
    return pl.pallas_call(
        functools.partial(_inproj_kernel, nt=nt),
        grid=(N_UNITS, nt + 1),
        in_specs=[pl.BlockSpec((tm, d), lambda j, i: (lat(i), 0)),
                  pl.BlockSpec((mc, d), lambda j, i: (0, 0)),
                  pl.BlockSpec((None, d, tn), lambda j, i: (layer, 0, j)),
                  tab, tab, tab, tab, vec, vec],
        out_specs=[pl.BlockSpec((tm, tn), lambda j, i: (lat(i), j)),
                   pl.BlockSpec((mc, tn), lambda j, i: (0, j))],
        out_shape=[jax.ShapeDtypeStruct((m, wid), BF16), jax.ShapeDtypeStruct((mc, wid), BF16)],
        scratch_shapes=[pltpu.VMEM((d, tn), BF16)],
        compiler_params=_cparams(("arbitrary", "arbitrary"), 56),
    )(hx, hc, w_all, tabs_a[0], tabs_a[1], tabs_b[0], tabs_b[1], gq.reshape(1, HEAD_DIM), gk.reshape(1, HEAD_DIM))


ATTN_TILE_ROWS = 1024
ATTN_SUB_ROWS = 256


def _attn_kernel(*refs, diff, has_lat, group, lam_init):
    it = iter(refs)
    q_ref, kc_ref, vc_ref = next(it), next(it), next(it)
    kl_ref = vl_ref = lam_ref = g_ref = None
    if has_lat:
        kl_ref, vl_ref = next(it), next(it)
    if diff:
        lam_ref, g_ref = next(it), next(it)
    o_ref, k_scr, v_scr = next(it), next(it), next(it)
    n_ctx = kc_ref.shape[0]

    @pl.when(pl.program_id(2) == 0)
    def _():
        k_scr[0:n_ctx, :] = kc_ref[...]
        v_scr[0:n_ctx, 0:HEAD_DIM] = vc_ref[...]
        if has_lat:
            k_scr[n_ctx:, :] = kl_ref[...]
            v_scr[n_ctx:, 0:HEAD_DIM] = vl_ref[...]
        v_scr[:, HEAD_DIM:] = jnp.ones((v_scr.shape[0], HEAD_DIM), BF16)

    k, v = k_scr[...], v_scr[...]

    def softmax_pv(q):
        s = _nt_dot(q, k)
        p = jnp.exp2(s - jnp.max(s, axis=-1, keepdims=True)).astype(BF16)
        oe = jnp.dot(p, v, preferred_element_type=F32)
        return oe[:, :HEAD_DIM] * (1.0 / oe[:, HEAD_DIM:])

    sub = min(ATTN_SUB_ROWS, q_ref.shape[0])
    for r0 in range(0, q_ref.shape[0], sub):
        rows = slice(r0, r0 + sub)
        for g in range(group):
            cols = slice(g * HEAD_DIM, (g + 1) * HEAD_DIM)
            q = q_ref[rows, cols]
            if diff:
                qf = q.astype(F32)
                lane = lax.broadcasted_iota(jnp.int32, qf.shape, 1)
                o1 = softmax_pv(jnp.where(lane < A_QK_DIM, qf, 0.0).astype(BF16))
                o2 = softmax_pv(jnp.where(lane >= A_QK_DIM, qf, 0.0).astype(BF16))
                lv = lam_ref[...]
                lam = (jnp.exp(jnp.sum(lv[0:1] * lv[1:2], axis=-1, keepdims=True))
                       - jnp.exp(jnp.sum(lv[2:3] * lv[3:4], axis=-1, keepdims=True)) + lam_init)
                o = _rms(o1 - lam * o2, g_ref[...]) * (1.0 - lam_init)
            else:
                o = softmax_pv(q)
            o_ref[rows, cols] = o.astype(o_ref.dtype)


def _attention(pq, pc, px, *, n_q, n_ctx, n_lat, kv_heads, group, q_unit, k_unit, v_unit, unit_heads,
               diff=False, lam=None, subln_g=None, lam_init=0.0):
    bsz = pq.shape[0] // n_q
    has_lat = px is not None
    tq = min(ATTN_TILE_ROWS * (2 if diff else 1), n_q)
    nqt = n_q // tq
    gw = group * HEAD_DIM
    n_keys = n_ctx + (n_lat if has_lat else 0)
    qc, kc, vc = q_unit * unit_heads // group, k_unit * unit_heads, v_unit * unit_heads
    assert (q_unit * unit_heads) % group == 0
    in_specs = [pl.BlockSpec((tq, gw), lambda b, h, i: (b * nqt + i, qc + h)),
                pl.BlockSpec((n_ctx, HEAD_DIM), lambda b, h, i: (b, kc + h)),
                pl.BlockSpec((n_ctx, HEAD_DIM), lambda b, h, i: (b, vc + h))]
    args = [pq, pc, pc]
    if has_lat:
        in_specs += [pl.BlockSpec((n_lat, HEAD_DIM), lambda b, h, i: (b, kc + h)),
                     pl.BlockSpec((n_lat, HEAD_DIM), lambda b, h, i: (b, vc + h))]
        args += [px, px]
    if diff:
        in_specs += [pl.BlockSpec((4, A_QK_DIM), lambda b, h, i: (0, 0)),
                     pl.BlockSpec((1, HEAD_DIM), lambda b, h, i: (0, 0))]
        args += [lam, subln_g.reshape(1, HEAD_DIM)]
    return pl.pallas_call(
        functools.partial(_attn_kernel, diff=diff, has_lat=has_lat, group=group, lam_init=lam_init),
        grid=(bsz, kv_heads, nqt),
        in_specs=in_specs,
        out_specs=pl.BlockSpec((tq, gw), lambda b, h, i: (b * nqt + i, h)),
        out_shape=jax.ShapeDtypeStruct((bsz * n_q, kv_heads * gw), BF16),
        scratch_shapes=[pltpu.VMEM((n_keys, HEAD_DIM), BF16), pltpu.VMEM((n_keys, 2 * HEAD_DIM), BF16)],
        compiler_params=_cparams(("parallel", "parallel", "arbitrary"), 48),
    )(*args)


NA_QROWS = 4


def _na_plan(rows):
    kh = min(NA_ROWS, rows)
    qb = min(NA_QROWS, rows)
    assert rows % qb == 0
    kw = min(rows, kh + qb)
    starts, var_ids, variants = [], [], []
    for r0 in range(0, rows, qb):
        rs = [int(np.clip(r - kh // 2, 0, rows - kh)) for r in range(r0, r0 + qb)]
        kstart = min(min(rs), rows - kw)
        assert kstart >= 0 and max(rs) + kh <= kstart + kw
        sig = tuple((r0 + u - kstart, rs[u] - kstart) for u in range(qb))
        if sig not in variants:
            variants.append(sig)
        starts.append(kstart)
        var_ids.append(variants.index(sig))
    return kw, starts, var_ids, variants


def _na_toeplitz(c_rpb):
    q = np.arange(GRID_W)
    kc = np.arange(GRID_W)
    start = np.clip(q - NA_COLS // 2, 0, GRID_W - NA_COLS)
    inwin = (kc[None, :] >= start[:, None]) & (kc[None, :] < start[:, None] + NA_COLS)
    ci = kc[None, :] - q[:, None] + NA_COLS - 1
    onehot = ((ci[..., None] == np.arange(2 * NA_COLS - 1)) & inwin[..., None]).astype(np.float32)
    toep = jnp.einsum("lhab,qcb->lhaqc", c_rpb.astype(F32), jnp.asarray(onehot), precision=lax.Precision.HIGHEST)
    return jnp.where(jnp.asarray(inwin)[None, None, None], toep * LOG2E, NEG)


def _na_kernel(kst_ref, var_ref, q_ref, k_ref, v_ref, kc_ref, vc_ref, toep_ref, o_ref, t_ref, va_ref, vca_ref, *,
               nblk, qlen, klen, variants, kh):
    kctx = kc_ref[...]
    va_ref[:, 0:HEAD_DIM] = v_ref[...]
    va_ref[:, HEAD_DIM:] = jnp.ones((va_ref.shape[0], HEAD_DIM), BF16)
    vca_ref[:, 0:HEAD_DIM] = vc_ref[...]
    vca_ref[:, HEAD_DIM:] = jnp.ones((vca_ref.shape[0], HEAD_DIM), BF16)
    vctx = vca_ref[...]

    masked = jnp.full((GRID_W, GRID_W), NEG, F32)
    for vi, sig in enumerate(variants):
        for u, (qoff, woff) in enumerate(sig):
            tile = lambda kr: toep_ref[kr - qoff + NA_ROWS - 1] if woff <= kr < woff + kh else masked
            for kr in range(0, klen // GRID_W, 2):
                t_ref[vi, u * GRID_W:(u + 1) * GRID_W, kr * GRID_W:(kr + 2) * GRID_W] = jnp.concatenate(
                    [tile(kr), tile(kr + 1)], axis=1)

    def body(rb, carry):
        qoff = pl.multiple_of(rb * qlen, qlen)
        koff = pl.multiple_of(kst_ref[rb] * GRID_W, GRID_W)
        q = q_ref[pl.ds(qoff, qlen), :]
        s_loc = _nt_dot(q, k_ref[pl.ds(koff, klen), :]) + t_ref[var_ref[rb]]
        s_ctx = _nt_dot(q, kctx)
        m = jnp.maximum(jnp.max(s_loc, axis=-1, keepdims=True), jnp.max(s_ctx, axis=-1, keepdims=True))
        p_loc = jnp.exp2(s_loc - m).astype(BF16)
        p_ctx = jnp.exp2(s_ctx - m).astype(BF16)
        oe = (jnp.dot(p_loc, va_ref[pl.ds(koff, klen), :], preferred_element_type=F32)
              + jnp.dot(p_ctx, vctx, preferred_element_type=F32))
        o_ref[pl.ds(qoff, qlen), :] = (oe[:, :HEAD_DIM] * (1.0 / oe[:, HEAD_DIM:])).astype(o_ref.dtype)
        return carry

    lax.fori_loop(0, nblk, body, 0, unroll=4 if nblk % 4 == 0 else 1)


def _neighbourhood(px, pc, toep, *, n_lat, n_ctx, heads, unit_heads):
    bsz = px.shape[0] // n_lat
    rows = n_lat // GRID_W
    kw, starts, var_ids, variants = _na_plan(rows)
    assert kw % 2 == 0
    nblk = len(starts)
    qlen, klen = n_lat // nblk, kw * GRID_W
    qc, kc, vc = UNIT_CQ * unit_heads, UNIT_CK * unit_heads, UNIT_CV * unit_heads
    lat = lambda c0: pl.BlockSpec((n_lat, HEAD_DIM), lambda b, h, ks, vi: (b, c0 + h))
    ctx = lambda c0: pl.BlockSpec((n_ctx, HEAD_DIM), lambda b, h, ks, vi: (b, c0 + h))
    return pl.pallas_call(
        functools.partial(_na_kernel, nblk=nblk, qlen=qlen, klen=klen, variants=tuple(variants),
                          kh=min(NA_ROWS, rows)),
        grid_spec=pltpu.PrefetchScalarGridSpec(
            num_scalar_prefetch=2,
            grid=(bsz, heads),
            in_specs=[lat(qc), lat(kc), lat(vc), ctx(kc), ctx(vc),
                      pl.BlockSpec((None,) + toep.shape[1:], lambda b, h, ks, vi: (h, 0, 0, 0))],
            out_specs=pl.BlockSpec((n_lat, HEAD_DIM), lambda b, h, ks, vi: (b, h)),
            scratch_shapes=[pltpu.VMEM((len(variants), qlen, klen), F32),
                            pltpu.VMEM((n_lat, 2 * HEAD_DIM), BF16), pltpu.VMEM((n_ctx, 2 * HEAD_DIM), BF16)]),
        out_shape=jax.ShapeDtypeStruct((bsz * n_lat, heads * HEAD_DIM), BF16),
        compiler_params=_cparams(("parallel", "parallel"), 40),
    )(jnp.asarray(starts, jnp.int32), jnp.asarray(var_ids, jnp.int32), px, px, px, pc, pc, toep)


def _outproj_kernel(oa_ref, ob_ref, oc_ref, w_ref, x_ref, g_ref, o_ref, wb_ref):
    @pl.when((pl.program_id(1) == 0) & (pl.program_id(2) == 0))
    def _():
        wb_ref[...] = w_ref[...].astype(BF16)

    wa, wb = oa_ref.shape[1], ob_ref.shape[1]
    acc = jnp.dot(oa_ref[...], wb_ref[0:wa, :], preferred_element_type=F32)
    acc += jnp.dot(ob_ref[...], wb_ref[wa:wa + wb, :], preferred_element_type=F32)
    acc += jnp.dot(oc_ref[...], wb_ref[wa + wb:, :], preferred_element_type=F32)
    o_ref[...] = x_ref[...] + g_ref[...] * acc


def _out_proj(oa, ob, oc, w, x, gate, row_fn):
    w_all, layer = w
    kdim = w_all.shape[1]
    bsz, n, d = x.shape
    tm = min(1024, n)
    tn = min(512, d)
    nt = n // tm
    lhs = lambda a: pl.BlockSpec((tm, a.shape[1]), lambda j, b, i: (b * nt + i, 0))
    return pl.pallas_call(
        _outproj_kernel,
        grid=(d // tn, bsz, nt),
        in_specs=[lhs(oa), lhs(ob), lhs(oc),
                  pl.BlockSpec((None, kdim, tn), lambda j, b, i: (layer, 0, j)),
                  pl.BlockSpec((None, tm, tn), lambda j, b, i: (b, i, j)),
                  pl.BlockSpec((None, 1, tn), lambda j, b, i: (row_fn(b), 0, j))],
        out_specs=pl.BlockSpec((None, tm, tn), lambda j, b, i: (b, i, j)),
        out_shape=jax.ShapeDtypeStruct((bsz, n, d), F32),
        scratch_shapes=[pltpu.VMEM((kdim, tn), BF16)],
        compiler_params=_cparams(("arbitrary", "arbitrary", "arbitrary"), 56),
    )(oa, ob, oc, w_all, x, gate)


def _split3(x):
    hi = x.astype(BF16)
    r = x - hi.astype(F32)
    mid = r.astype(BF16)
    lo = (r - mid.astype(F32)).astype(BF16)
    return hi, mid, lo


def _router_kernel(x_ref, g_ref, sh_ref, sc_ref, wh_ref, wl_ref, h_ref, afft_ref, aff_ref):
    y = _rms(x_ref[...], g_ref[...])
    h = y * (1.0 + sc_ref[...]) + sh_ref[...]
    h_ref[...] = h
    hh = h.astype(BF16)
    hl = (h - hh.astype(F32)).astype(BF16)
    wh, wl = wh_ref[...], wl_ref[...]
    lt = _nt_dot(wh, hh) + _nt_dot(wh, hl) + _nt_dot(wl, hh)
    e = jnp.exp(lt - jnp.max(lt, axis=0, keepdims=True))
    afft_ref[...] = e / jnp.sum(e, axis=0, keepdims=True)
    ln = _nt_dot(hh, wh) + _nt_dot(hl, wh) + _nt_dot(hh, wl)
    en = jnp.exp(ln - jnp.max(ln, axis=-1, keepdims=True))
    aff_ref[...] = en / jnp.sum(en, axis=-1, keepdims=True)


def _router(x, g, shift, scale, row_fn, w_router):
    bsz, n, d = x.shape
    ne = w_router.shape[1]
    tm = min(256, n)
    wt = w_router.T
    wh = wt.astype(BF16)
    wl = (wt - wh.astype(F32)).astype(BF16)
    wspec = pl.BlockSpec((ne, d), lambda b, i: (0, 0))
    return pl.pallas_call(
        _router_kernel,
        grid=(bsz, n // tm),
        in_specs=[pl.BlockSpec((None, tm, d), lambda b, i: (b, i, 0)),
                  pl.BlockSpec((1, d), lambda b, i: (0, 0)),
                  _row_spec(d, row_fn), _row_spec(d, row_fn), wspec, wspec],
        out_specs=[pl.BlockSpec((None, tm, d), lambda b, i: (b, i, 0)),
                   pl.BlockSpec((None, ne, tm), lambda b, i: (b, 0, i)),
                   pl.BlockSpec((None, tm, ne), lambda b, i: (b, i, 0))],
        out_shape=[jax.ShapeDtypeStruct((bsz, n, d), F32),
                   jax.ShapeDtypeStruct((bsz, ne, n), F32),
                   jax.ShapeDtypeStruct((bsz, n, ne), F32)],
        compiler_params=_cparams(("parallel", "parallel"), 40),
    )(x, g.reshape(1, d), shift, scale, wh, wl)


def _lane_cumsum(x01, tri):
    r, n = x01.shape
    carry = jnp.zeros((r, 1), F32)
    out = []
    for c in range(n // HEAD_DIM):
        blk = x01[:, c * HEAD_DIM:(c + 1) * HEAD_DIM].astype(BF16)
        cs = jnp.dot(blk, tri, preferred_element_type=F32) + carry
        out.append(cs)
        carry = cs[:, HEAD_DIM - 1:HEAD_DIM]
    return jnp.concatenate(out, axis=1) if len(out) > 1 else out[0]


def _select_kernel(afft_ref, r_ref, o_ref, *, cap):
    a = afft_ref[...]
    ne, n = a.shape
    bits = pltpu.bitcast(a, jnp.int32)

    def bisect(_, lohi):
        lo, hi = lohi
        mid = lo + lax.shift_right_logical(hi - lo, 1)
        cnt = jnp.sum((bits >= mid).astype(F32), axis=1, keepdims=True)
        ok = cnt >= cap
        return jnp.where(ok, mid, lo), jnp.where(ok, hi, mid)

    lo0 = jnp.zeros((ne, 1), jnp.int32)
    hi0 = jnp.full((ne, 1), 0x7F800000, jnp.int32)
    thr, _ = lax.fori_loop(0, 32, bisect, (lo0, hi0))

    ri = lax.broadcasted_iota(jnp.int32, (HEAD_DIM, HEAD_DIM), 0)
    ci = lax.broadcasted_iota(jnp.int32, (HEAD_DIM, HEAD_DIM), 1)
    tri = (ri <= ci).astype(BF16)
    gt = (bits > thr).astype(F32)
    eq = (bits == thr).astype(F32)
    need = cap - jnp.sum(gt, axis=1, keepdims=True)
    sel = gt + eq * (_lane_cumsum(eq, tri) <= need).astype(F32)
    pos = _lane_cumsum(sel, tri) - 1.0

    slot = lax.broadcasted_iota(jnp.int32, (cap, n), 0).astype(F32)
    rmat = r_ref[...]
    for e in range(ne):
        onehot = jnp.where((pos[e:e + 1, :] == slot) & (sel[e:e + 1, :] > 0.5), 1.0, 0.0).astype(BF16)
        o_ref[e] = jnp.dot(onehot, rmat, preferred_element_type=F32)


IDX_RADIX = 64


def _select(afft, aff, cap):
    bsz, ne, n = afft.shape
    t = jnp.arange(n)
    hi, mid, lo = _split3(aff)
    assert n <= IDX_RADIX * 256
    tcols = jnp.stack([t // IDX_RADIX, t % IDX_RADIX], axis=1).astype(BF16)
    pad = jnp.zeros((bsz, n, HEAD_DIM - 2 - 3 * ne), BF16)
    rmat = jnp.concatenate([jnp.broadcast_to(tcols[None], (bsz, n, 2)), hi, mid, lo, pad], axis=2)
    tab = pl.pallas_call(
        functools.partial(_select_kernel, cap=cap),
        grid=(bsz,),
        in_specs=[pl.BlockSpec((None, ne, n), lambda b: (b, 0, 0)),
                  pl.BlockSpec((None, n, HEAD_DIM), lambda b: (b, 0, 0))],
        out_specs=pl.BlockSpec((None, ne, cap, HEAD_DIM), lambda b: (b, 0, 0, 0)),
        out_shape=jax.ShapeDtypeStruct((bsz, ne, cap, HEAD_DIM), F32),
        compiler_params=_cparams(("parallel",), 40),
    )(afft, rmat)
    idx = (tab[..., 0] * float(IDX_RADIX) + tab[..., 1]).astype(jnp.int32)
    parts = tab[..., 2:2 + 3 * ne].reshape(bsz, ne, cap, 3, ne).sum(axis=3)
    gates = (parts * jnp.eye(ne, dtype=F32)[None, :, None, :]).sum(axis=-1, keepdims=True)
    return idx, gates


def _ffn_up_kernel(*refs, caps, ne, nb):
    ns = len(caps)
    idx_refs, h_refs = refs[:ns], refs[ns:2 * ns]
    wg_ref, wu_ref, o_ref, xg_ref, sem = refs[2 * ns:]
    rows = sum(caps)
    e, b = pl.program_id(0), pl.program_id(1)
    step = e * nb + b
    slot = lax.rem(step, 2)

    def gather(e_, b_, slot_):
        r0 = 0
        for idx_ref, h_hbm, cap in zip(idx_refs, h_refs, caps):
            base = (b_ * ne + e_) * cap
            for j in range(cap):
                pltpu.make_async_copy(h_hbm.at[b_, pl.ds(idx_ref[base + j], 1), :],
                                      xg_ref.at[slot_, pl.ds(r0 + j, 1), :], sem.at[slot_]).start(priority=j % 2)
            r0 += cap

    def wait_all(slot_):
        pltpu.make_async_copy(xg_ref.at[slot_], xg_ref.at[slot_], sem.at[slot_]).wait()

    @pl.when(step == 0)
    def _():
        gather(e, b, 0)

    wait_all(slot)

    last = step + 1 >= ne * nb
    nxt = jnp.where(last, step, step + 1)
    gather(lax.div(nxt, nb), lax.rem(nxt, nb), 1 - slot)

    x = xg_ref[slot].astype(BF16)
    gate = jnp.dot(x, wg_ref[...].astype(BF16), preferred_element_type=F32)
    up = jnp.dot(x, wu_ref[...].astype(BF16), preferred_element_type=F32)
    o_ref[...] = (gate * jax.nn.sigmoid(gate) * up).astype(o_ref.dtype)

    @pl.when(last)
    def _():
        wait_all(1 - slot)


def _ffn_up(streams, wg, wu):
    caps = tuple(c for _, _, c in streams)
    bsz, _, d = streams[0][1].shape
    (wg_all, layer), (wu_all, _) = wg, wu
    ne, _, ff = wg_all.shape[1:]
    ns = len(streams)
    imap = lambda f: (lambda e, b, *idx: f(e, b))
    wspec = pl.BlockSpec((None, None, d, ff), imap(lambda e, b: (layer, e, 0, 0)))
    rows = sum(caps)
    return pl.pallas_call(
        functools.partial(_ffn_up_kernel, caps=caps, ne=ne, nb=bsz),
        grid_spec=pltpu.PrefetchScalarGridSpec(
            num_scalar_prefetch=ns,
            grid=(ne, bsz),
            in_specs=[pl.BlockSpec(memory_space=pl.ANY)] * ns + [wspec, wspec],
            out_specs=pl.BlockSpec((None, None, rows, ff), imap(lambda e, b: (b, e, 0, 0))),
            scratch_shapes=[pltpu.VMEM((2, rows, d), F32), pltpu.SemaphoreType.DMA((2,))]),
        out_shape=jax.ShapeDtypeStruct((bsz, ne, rows, ff), BF16),
        compiler_params=_cparams(("arbitrary", "arbitrary"), 58),
    )(*[i for i, _, _ in streams], *[h for _, h, _ in streams], wg_all, wu_all)


ACC_FULL_WIDTH_BYTES = 8 * MIB
SCATTER_ROWS = 8


def _ffn_down_kernel(idx_ref, hid_ref, wd_ref, gate_ref, x_ref, g2_ref, o_ref, acc_ref, ye_ref, *, cap, row0, ne, tr):
    b, e = pl.program_id(0), pl.program_id(2)

    @pl.when(e == 0)
    def _():
        acc_ref[...] = jnp.zeros_like(acc_ref)

    @pl.when(e < ne)
    def _():
        hid = hid_ref[row0:row0 + cap, :]
        ye_ref[...] = jnp.dot(hid, wd_ref[...].astype(BF16), preferred_element_type=F32) * gate_ref[...]
        base = (b * ne + e) * cap

        for j0 in range(0, cap, SCATTER_ROWS):
            toks = [idx_ref[base + j0 + u] for u in range(SCATTER_ROWS)]
            rows = [acc_ref[pl.ds(t, 1), :] for t in toks]
            for u, (t, r) in enumerate(zip(toks, rows)):
                acc_ref[pl.ds(t, 1), :] = r + ye_ref[j0 + u:j0 + u + 1, :]

    @pl.when(e >= ne)
    def _():
        r0 = pl.multiple_of((e - ne) * tr, tr)
        o_ref[...] = x_ref[...] + g2_ref[...] * acc_ref[pl.ds(r0, tr), :]


def _ffn_down(idx_flat, hid, row0, wd, gates, x, g2, row_fn):
    bsz, n, d = x.shape
    wd_all, layer = wd
    ne, ff = wd_all.shape[1:3]
    cap = gates.shape[2]
    rows = hid.shape[2]
    assert cap % SCATTER_ROWS == 0
    cb = d if n * d * 4 <= ACC_FULL_WIDTH_BYTES else min(1024, d)
    tr = min(512, n)
    ex = lambda e: jnp.minimum(e, ne - 1)
    rt = lambda e: jnp.maximum(e - ne, 0)
    return pl.pallas_call(
        functools.partial(_ffn_down_kernel, cap=cap, row0=row0, ne=ne, tr=tr),
        grid_spec=pltpu.PrefetchScalarGridSpec(
            num_scalar_prefetch=1,
            grid=(bsz, d // cb, ne + n // tr),
            in_specs=[pl.BlockSpec((None, None, rows, ff), lambda b, c, e, idx: (b, ex(e), 0, 0)),
                      pl.BlockSpec((None, None, ff, cb), lambda b, c, e, idx: (layer, ex(e), 0, c)),
                      pl.BlockSpec((None, None, cap, 1), lambda b, c, e, idx: (b, ex(e), 0, 0)),
                      pl.BlockSpec((None, tr, cb), lambda b, c, e, idx: (b, rt(e), c)),
                      pl.BlockSpec((None, 1, cb), lambda b, c, e, idx: (row_fn(b), 0, c))],
            out_specs=pl.BlockSpec((None, tr, cb), lambda b, c, e, idx: (b, rt(e), c)),
            scratch_shapes=[pltpu.VMEM((n, cb), F32), pltpu.VMEM((cap, cb), F32)]),
        out_shape=jax.ShapeDtypeStruct((bsz, n, d), F32),
        compiler_params=_cparams(("parallel", "parallel", "arbitrary"), 48),
    )(idx_flat, hid, wd_all, gates, x, g2)


def _route(x, norm_g, shift, scale, row_fn, w_router):
    n = x.shape[1]
    cap = EC_CAPACITY * n // w_router.shape[1]
    h, afft, aff = _router(x, norm_g, shift, scale, row_fn, w_router)
    idx, gates = _select(afft, aff, cap)
    return h, idx.reshape(-1), gates, cap


def _expert_ffn_residual(streams, norm_g, shift, scale, g2, w_router, wg, wu, wd):
    routed = [_route(x, norm_g, shift, scale, row_fn, w_router) for x, row_fn in streams]
    hid = _ffn_up([(idx, h, cap) for h, idx, _, cap in routed], wg, wu)
    outs, row0 = [], 0
    for (x, row_fn), (_, idx, gates, cap) in zip(streams, routed):
        outs.append(_ffn_down(idx, hid, row0, wd, gates, x, g2, row_fn))
        row0 += cap
    return outs


def kernel(x, c, ctx, c_ctx, w_ada, b_ada, norm1_g, norm2_g, w_in, w_out, a_lambda, a_subln_g, b_q_norm_g,
           b_k_norm_g, c_rpb, w_router, w_e_gate, w_e_up, w_e_down, final_g):
    bsz, n, d = x.shape
    n_ctx = ctx.shape[1]
    depth = w_ada.shape[0]
    rows = n // GRID_W
    unit_heads = d // 8 // HEAD_DIM
    a_heads, b_heads, c_heads = 2 * unit_heads, 3 * unit_heads, 3 * unit_heads
    assert bsz + 1 <= MOD_ROWS and rows >= 1 and n % GRID_W == 0

    cvec = jnp.zeros((MOD_ROWS, d), F32).at[:bsz].set(c).at[bsz].set(c_ctx)
    mod = _ada_mod(cvec, w_ada, b_ada)
    lat_row = lambda b: b
    ctx_row = lambda b: bsz

    tabs_a, tabs_b = _rope_tables(n, "A"), _rope_tables(n, "B")
    na_toep = _na_toeplitz(c_rpb)

    for l in range(depth):
        need_ctx = l < depth - 1
        lam_init = 0.8 - 0.6 * float(np.exp(-0.3 * l))
        m6 = [mod[l, :, k * d:(k + 1) * d].reshape(MOD_ROWS, 1, d) for k in range(N_MOD)]
        sh1, sc1, g1, sh2, sc2, g2 = m6
        w_in_l, w_out_l = (w_in, l), (w_out, l)
        wg, wu, wd = (w_e_gate, l), (w_e_up, l), (w_e_down, l)

        hx = _norm_mod(x, norm1_g[l], sh1, sc1, lat_row, BF16).reshape(bsz * n, d)
        hc = _norm_mod(ctx, norm1_g[l], sh1, sc1, ctx_row, BF16).reshape(bsz * n_ctx, d)
        px, pc = _in_proj(hx, hc, w_in_l, tabs_a, tabs_b, b_q_norm_g[l], b_k_norm_g[l], n)

        common = dict(n_ctx=n_ctx, n_lat=n, unit_heads=unit_heads)
        a_args = dict(kv_heads=a_heads, group=1, q_unit=UNIT_AQ, k_unit=UNIT_AK, v_unit=UNIT_AV, diff=True,
                      lam=a_lambda[l], subln_g=a_subln_g[l], lam_init=lam_init)
        b_args = dict(kv_heads=unit_heads, group=b_heads // unit_heads, q_unit=UNIT_BQ, k_unit=UNIT_BK, v_unit=UNIT_BV)
        oa = _attention(px, pc, px, n_q=n, **a_args, **common)
        ob = _attention(px, pc, px, n_q=n, **b_args, **common)
        oc = _neighbourhood(px, pc, na_toep[l], n_lat=n, n_ctx=n_ctx, heads=c_heads, unit_heads=unit_heads)
        x = _out_proj(oa, ob, oc, w_out_l, x, g1, lat_row)
        streams = [(x, lat_row)]
        if need_ctx:
            c_args = dict(kv_heads=c_heads, group=1, q_unit=UNIT_CQ, k_unit=UNIT_CK, v_unit=UNIT_CV)
            oa_c = _attention(pc, pc, None, n_q=n_ctx, **a_args, **common)
            ob_c = _attention(pc, pc, None, n_q=n_ctx, **b_args, **common)
            oc_c = _attention(pc, pc, None, n_q=n_ctx, **c_args, **common)
            ctx = _out_proj(oa_c, ob_c, oc_c, w_out_l, ctx, g1, ctx_row)
            streams.append((ctx, ctx_row))
        outs = _expert_ffn_residual(streams, norm2_g[l], sh2, sc2, g2, w_router[l], wg, wu, wd)
        x = outs[0]
        if need_ctx:
            ctx = outs[1]

    return _final_norm(x, final_g)
```
